```python
import math
import jax, jax.numpy as jnp
from jax import lax
import numpy as np

D_MODEL = 2048
BATCH = 4
SEQ = 2048
DEPTH = 1
DEC_BATCH = 128
DEC_SEQ = 8
PAST_LEN = 2048
PAGE_SIZE = 128

HEAD_DIM = 128
N_ATTN_HEADS = D_MODEL // HEAD_DIM
ATTN_WIDTH = N_ATTN_HEADS * HEAD_DIM
MOBA_BLOCK = 256
MOBA_TOPK = 3
MOBA_Q_BLOCK = 32
ROPE_THETA = 10000.0
N_DN_HEADS = D_MODEL // 128
DN_KEY_DIM = 128
DN_VAL_DIM = 128
DN_K_WIDTH = N_DN_HEADS * DN_KEY_DIM
DN_V_WIDTH = N_DN_HEADS * DN_VAL_DIM
DN_CONV_CH = 2 * DN_K_WIDTH + DN_V_WIDTH
CONV_WIDTH = 4
DN_CHUNK = 64
PLE_DIM = 256
NORM_EPS = 1e-6
IN_SIZES = (ATTN_WIDTH, ATTN_WIDTH, ATTN_WIDTH, ATTN_WIDTH,
            DN_CONV_CH, DN_V_WIDTH, N_DN_HEADS, N_DN_HEADS, D_MODEL, D_MODEL)
N_IN = sum(IN_SIZES)

kernel_name = "hybrid_moba_gdn_decode_step"


def split_points():
    pts, acc = [], 0
    for s in IN_SIZES[:-1]:
        acc += s
        pts.append(acc)
    return pts


def rmsnorm(x, w):
    xf = x.astype(jnp.float32)
    y = xf * lax.rsqrt(jnp.mean(xf * xf, -1, keepdims=True) + NORM_EPS)
    return (y * w.astype(jnp.float32)).astype(x.dtype)


def l2norm(x):
    xf = x.astype(jnp.float32)
    return (xf * lax.rsqrt(jnp.sum(xf * xf, -1, keepdims=True) + NORM_EPS)).astype(x.dtype)


def rope(x, pos):
    half = HEAD_DIM // 2
    inv_freq = jnp.power(ROPE_THETA, -jnp.arange(half, dtype=jnp.float32) / half)
    ang = pos.astype(jnp.float32)[:, None] * inv_freq[None, :]
    cos = jnp.cos(ang)[None, :, None, :]
    sin = jnp.sin(ang)[None, :, None, :]
    xf = x.astype(jnp.float32)
    x1, x2 = xf[..., :half], xf[..., half:]
    return jnp.concatenate([x1 * cos - x2 * sin, x2 * cos + x1 * sin], -1).astype(x.dtype)


def causal_conv_silu(x, buf, w):
    L = x.shape[1]
    xp = jnp.concatenate([buf.astype(x.dtype), x], 1)
    y = sum(xp[:, j:j + L] * w[j] for j in range(CONV_WIDTH))
    return jax.nn.silu(y), xp[:, L:]


def gated_delta_rule(q, k, v, g, beta, S0):
    B, L, H, _ = q.shape
    DV = v.shape[-1]
    C = math.gcd(L, DN_CHUNK)
    NC = L // C

    def chunks(t):
        t = t.astype(jnp.float32).reshape((B, NC, C, H) + t.shape[3:])
        return jnp.moveaxis(jnp.swapaxes(t, 2, 3), 1, 0)

    qc, kc, vc, gc, bc = chunks(q), chunks(k), chunks(v), chunks(g), chunks(beta)
    G = jnp.cumsum(gc, -1)
    incl = jnp.tril(jnp.ones((C, C), bool))
    strict = jnp.tril(jnp.ones((C, C), bool), -1)
    decay = jnp.exp(jnp.where(incl, G[..., :, None] - G[..., None, :], -jnp.inf))
    A = jnp.where(strict, bc[..., :, None] * jnp.einsum('nbhcd,nbhed->nbhce', kc, kc) * decay, 0.0)
    rhs = jnp.concatenate([vc * bc[..., None], kc * (bc * jnp.exp(G))[..., None]], -1)
    sol = lax.linalg.triangular_solve(A + jnp.eye(C, dtype=jnp.float32), rhs,
                                      left_side=True, lower=True, unit_diagonal=True)
    U, W = sol[..., :DV], sol[..., DV:]
    qk = jnp.einsum('nbhcd,nbhed->nbhce', qc, kc) * decay

    def step(S, xs):
        qi, ki, Ui, Wi, Gi, qki = xs
        v_new = Ui - jnp.einsum('bhcd,bhde->bhce', Wi, S)
        o = (jnp.einsum('bhcd,bhde->bhce', qi * jnp.exp(Gi)[..., None], S)
             + jnp.einsum('bhce,bhef->bhcf', qki, v_new))
        g_last = Gi[..., -1:]
        S = (S * jnp.exp(g_last)[..., None]
             + jnp.einsum('bhcd,bhce->bhde', ki * jnp.exp(g_last - Gi)[..., None], v_new))
        return S, o

    S, o = lax.scan(step, S0.astype(jnp.float32), (qc, kc, U, W, G, qk))
    o = jnp.swapaxes(jnp.moveaxis(o, 0, 1), 2, 3).reshape(B, L, H, DV)
    return o, S


def seq_blocks(t):
    S_pad, H, HD = t.shape
    return t.reshape(S_pad // MOBA_BLOCK, MOBA_BLOCK, H, HD).transpose(2, 0, 1, 3)


def moba_attend(q, q_pos, kb, vb, kmean):
    Lq, H, _ = q.shape
    NB = kb.shape[1]
    own = q_pos // MOBA_BLOCK
    gate = jnp.einsum('qhd,hnd->qhn', q.astype(jnp.float32), kmean)
    past = jnp.arange(NB)[None, None, :] < own[:, None, None]
    gate = jnp.where(past, gate, -jnp.inf)
    if NB < MOBA_TOPK:
        gate = jnp.pad(gate, ((0, 0), (0, 0), (0, MOBA_TOPK - NB)), constant_values=-jnp.inf)
    _, top = lax.top_k(gate, MOBA_TOPK)
    top_valid = jnp.arange(MOBA_TOPK)[None, None, :] < own[:, None, None]
    own_b = jnp.broadcast_to(own[:, None, None], (Lq, H, 1))
    blk = jnp.concatenate([jnp.where(top_valid, top, own_b), own_b], -1)
    valid = jnp.concatenate([jnp.broadcast_to(top_valid, top.shape), jnp.ones_like(own_b, dtype=bool)], -1)
    heads = jnp.arange(H)[None, :, None]
    k_sel = kb[heads, blk]
    v_sel = vb[heads, blk]
    key_pos = blk[..., None] * MOBA_BLOCK + jnp.arange(MOBA_BLOCK)
    mask = valid[..., None] & (key_pos <= q_pos[:, None, None, None])
    logits = jnp.einsum('qhd,qhsjd->qhsj', q, k_sel, preferred_element_type=jnp.float32) * (HEAD_DIM ** -0.5)
    logits = jnp.where(mask, logits, -jnp.inf)
    probs = jax.nn.softmax(logits.reshape(Lq, H, -1), -1).reshape(logits.shape)
    return jnp.einsum('qhsj,qhsjd->qhd', probs.astype(v_sel.dtype), v_sel)


def moba_prompt(q, k, v):
    B, L, H, HD = q.shape
    S_pad = -(-L // MOBA_BLOCK) * MOBA_BLOCK
    pad = ((0, 0), (0, S_pad - L), (0, 0), (0, 0))
    NB = S_pad // MOBA_BLOCK
    kb = jnp.pad(k, pad).reshape(B, NB, MOBA_BLOCK, H, HD).transpose(0, 3, 1, 2, 4)
    vb = jnp.pad(v, pad).reshape(B, NB, MOBA_BLOCK, H, HD).transpose(0, 3, 1, 2, 4)
    kmean = kb.astype(jnp.float32).mean(3)
    qbs = math.gcd(L, MOBA_Q_BLOCK)
    nqb = L // qbs

    def item(i):
        b = i // nqb
        start = (i % nqb) * qbs
        qi = lax.dynamic_slice_in_dim(q[b], start, qbs, 0)
        return moba_attend(qi, start + jnp.arange(qbs), kb[b], vb[b], kmean[b])

    return lax.map(item, jnp.arange(B * nqb)).reshape(B, L, H, HD)


def moba_sample(q, k, v, cache_k, cache_v, page_table):
    B, L, H, HD = q.shape
    S_tot = PAST_LEN + L
    S_pad = -(-S_tot // MOBA_BLOCK) * MOBA_BLOCK
    pos = PAST_LEN + jnp.arange(L)
    zpad = jnp.zeros((S_pad - S_tot, H, HD), k.dtype)

    def item(b):
        pages = page_table[b]
        k_seq = jnp.concatenate([cache_k[pages].reshape(PAST_LEN, H, HD).astype(k.dtype), k[b], zpad], 0)
        v_seq = jnp.concatenate([cache_v[pages].reshape(PAST_LEN, H, HD).astype(v.dtype), v[b], zpad], 0)
        kb, vb = seq_blocks(k_seq), seq_blocks(v_seq)
        return moba_attend(q[b], pos, kb, vb, kb.astype(jnp.float32).mean(2))

    return lax.map(item, jnp.arange(B))


def layer(x, p_i, pos, conv_buf, S0, attend, norm_w, w_in, q_norm_w, k_norm_w, conv_w, a_log, dt_bias,
          dn_norm_w, w_o_attn, w_o_delta, w_out, ple_proj, ple_gate, ple_norm_w):
    B, L, _ = x.shape
    h = rmsnorm(x, norm_w)
    proj = h @ w_in
    qa, ka, va, za, qkv_b, zb, b_raw, a_raw, ga, gb = jnp.split(proj, split_points(), -1)
    qa = rope(rmsnorm(qa.reshape(B, L, N_ATTN_HEADS, HEAD_DIM), q_norm_w), pos)
    ka = rope(rmsnorm(ka.reshape(B, L, N_ATTN_HEADS, HEAD_DIM), k_norm_w), pos)
    va = va.reshape(B, L, N_ATTN_HEADS, HEAD_DIM)
    oa = attend(qa, ka, va).reshape(B, L, ATTN_WIDTH) * jax.nn.silu(za)
    xc, new_conv = causal_conv_silu(qkv_b, conv_buf, conv_w)
    qd, kd, vd = jnp.split(xc, [DN_K_WIDTH, 2 * DN_K_WIDTH], -1)
    qd = l2norm(qd.reshape(B, L, N_DN_HEADS, DN_KEY_DIM)) * (DN_KEY_DIM ** -0.5)
    kd = l2norm(kd.reshape(B, L, N_DN_HEADS, DN_KEY_DIM))
    vd = vd.reshape(B, L, N_DN_HEADS, DN_VAL_DIM)
    beta = jax.nn.sigmoid(b_raw.astype(jnp.float32))
    g = -jnp.exp(a_log.astype(jnp.float32)) * jax.nn.softplus(a_raw.astype(jnp.float32) + dt_bias.astype(jnp.float32))
    od, S_new = gated_delta_rule(qd, kd, vd, g, beta, S0)
    od = rmsnorm(od, dn_norm_w).reshape(B, L, DN_V_WIDTH).astype(x.dtype) * jax.nn.silu(zb)
    merged = jax.nn.sigmoid(ga) * (oa @ w_o_attn) + jax.nn.sigmoid(gb) * (od @ w_o_delta)
    x = x + merged @ w_out
    x = x + jax.nn.sigmoid(rmsnorm(x, ple_norm_w) @ ple_gate) * (p_i.astype(x.dtype) @ ple_proj)
    return x, ka, va, S_new.astype(S0.dtype), new_conv


def setup_inputs(seed: int = 0) -> dict:
    key = jax.random.key(seed)
    ks = jax.random.split(key, 24)
    n_pages = PAST_LEN // PAGE_SIZE
    n_used = DEC_BATCH * n_pages
    n_pool = (n_used * 5) // 4
    nrm = jax.random.normal
    f32 = jnp.float32
    dt = jnp.exp(jax.random.uniform(ks[12], (DEPTH, N_DN_HEADS), f32) * (math.log(0.1) - math.log(0.001)) + math.log(0.001))
    return {
        "x_prompt": nrm(ks[0], (BATCH, SEQ, D_MODEL), f32),
        "x_sample": nrm(ks[1], (DEC_BATCH, DEC_SEQ, D_MODEL), f32),
        "cache_k": nrm(ks[2], (DEPTH, n_pool, PAGE_SIZE, N_ATTN_HEADS, HEAD_DIM), f32),
        "cache_v": nrm(ks[3], (DEPTH, n_pool, PAGE_SIZE, N_ATTN_HEADS, HEAD_DIM), f32),
        "state_delta": 0.1 * nrm(ks[4], (DEPTH, DEC_BATCH, N_DN_HEADS, DN_KEY_DIM, DN_VAL_DIM), f32),
        "state_conv": nrm(ks[5], (DEPTH, DEC_BATCH, CONV_WIDTH - 1, DN_CONV_CH), f32),
        "page_table": jax.random.permutation(ks[6], n_pool)[:n_used].reshape(DEC_BATCH, n_pages).astype(jnp.int32),
        "p_prompt": nrm(ks[7], (DEPTH, BATCH, SEQ, PLE_DIM), f32),
        "p_sample": nrm(ks[8], (DEPTH, DEC_BATCH, DEC_SEQ, PLE_DIM), f32),
        "norm_w": 1.0 + 0.02 * nrm(ks[9], (DEPTH, D_MODEL), f32),
        "w_in": nrm(ks[10], (DEPTH, D_MODEL, N_IN), f32) * D_MODEL ** -0.5,
        "q_norm_w": 1.0 + 0.02 * nrm(ks[11], (DEPTH, HEAD_DIM), f32),
        "k_norm_w": 1.0 + 0.02 * nrm(ks[13], (DEPTH, HEAD_DIM), f32),
        "conv_w": nrm(ks[14], (DEPTH, CONV_WIDTH, DN_CONV_CH), f32) * CONV_WIDTH ** -0.5,
        "a_log": jnp.log(jax.random.uniform(ks[15], (DEPTH, N_DN_HEADS), f32, 1.0, 16.0)),
        "dt_bias": dt + jnp.log(-jnp.expm1(-dt)),
        "dn_norm_w": 1.0 + 0.02 * nrm(ks[16], (DEPTH, DN_VAL_DIM), f32),
        "w_o_attn": nrm(ks[17], (DEPTH, ATTN_WIDTH, D_MODEL), f32) * ATTN_WIDTH ** -0.5,
        "w_o_delta": nrm(ks[18], (DEPTH, DN_V_WIDTH, D_MODEL), f32) * DN_V_WIDTH ** -0.5,
        "w_out": nrm(ks[19], (DEPTH, D_MODEL, D_MODEL), f32) * D_MODEL ** -0.5,
        "ple_proj": nrm(ks[20], (DEPTH, PLE_DIM, D_MODEL), f32) * PLE_DIM ** -0.5,
        "ple_gate": nrm(ks[21], (DEPTH, D_MODEL, D_MODEL), f32) * D_MODEL ** -0.5,
        "ple_norm_w": 1.0 + 0.02 * nrm(ks[22], (DEPTH, D_MODEL), f32),
    }


def reference(x_prompt, x_sample, cache_k, cache_v, state_delta, state_conv, page_table, p_prompt, p_sample,
              norm_w, w_in, q_norm_w, k_norm_w, conv_w, a_log, dt_bias, dn_norm_w, w_o_attn, w_o_delta, w_out,
              ple_proj, ple_gate, ple_norm_w):
    B, L = x_prompt.shape[:2]
    pos_p = jnp.arange(L)
    pos_s = PAST_LEN + jnp.arange(x_sample.shape[1])
    yp, ys = x_prompt, x_sample
    kp_l, vp_l, sp_l, cp_l, ks_l, vs_l, ss_l, cs_l = [], [], [], [], [], [], [], []
    for i in range(DEPTH):
        lw = (norm_w[i], w_in[i], q_norm_w[i], k_norm_w[i], conv_w[i], a_log[i], dt_bias[i], dn_norm_w[i],
              w_o_attn[i], w_o_delta[i], w_out[i], ple_proj[i], ple_gate[i], ple_norm_w[i])
        conv0 = jnp.zeros((B, CONV_WIDTH - 1, DN_CONV_CH), x_prompt.dtype)
        S0 = jnp.zeros((B, N_DN_HEADS, DN_KEY_DIM, DN_VAL_DIM), state_delta.dtype)
        yp, kp, vp, sp, cp = layer(yp, p_prompt[i], pos_p, conv0, S0, moba_prompt, *lw)
        attend_s = lambda q, k, v, ck=cache_k[i], cv=cache_v[i]: moba_sample(q, k, v, ck, cv, page_table)
        ys, kss, vss, sss, css = layer(ys, p_sample[i], pos_s, state_conv[i], state_delta[i], attend_s, *lw)
        kp_l.append(kp); vp_l.append(vp); sp_l.append(sp); cp_l.append(cp)
        ks_l.append(kss); vs_l.append(vss); ss_l.append(sss); cs_l.append(css)
    k_prompt, v_prompt = jnp.stack(kp_l), jnp.stack(vp_l)
    delta_prompt, conv_prompt = jnp.stack(sp_l), jnp.stack(cp_l)
    k_sample, v_sample = jnp.stack(ks_l), jnp.stack(vs_l)
    delta_sample, conv_sample = jnp.stack(ss_l), jnp.stack(cs_l)
    return (yp, ys, k_prompt, v_prompt, delta_prompt, conv_prompt, k_sample, v_sample, delta_sample, conv_sample)
```

```python
import functools
import math

import jax
import jax.numpy as jnp
from jax import lax
from jax.experimental import pallas as pl
from jax.experimental.pallas import tpu as pltpu

F32 = jnp.float32
BF16 = jnp.bfloat16
HIGHEST = lax.Precision.HIGHEST

D_MODEL = 2048
HEAD_DIM = 128
N_HEADS = 16
WIDTH = N_HEADS * HEAD_DIM
MOBA_BLOCK = 256
MOBA_TOPK = 3
ROPE_THETA = 10000.0
CONV_CH = 3 * WIDTH
CONV_WIDTH = 4
PLE_DIM = 256
NORM_EPS = 1e-6
PAST_LEN = 2048
PAGE_SIZE = 128
PAGE_ROWS = PAGE_SIZE * N_HEADS
N_MAIN = 4 * WIDTH + CONV_CH + WIDTH + 2 * D_MODEL
LANES = 128
VMEM_LIMIT = 56 * 1024 * 1024

NT_DIMS = (((1,), (1,)), ((), ()))
TN_DIMS = (((0,), (0,)), ((), ()))


def _params(*sem):
    return pltpu.CompilerParams(dimension_semantics=sem, vmem_limit_bytes=VMEM_LIMIT)


def _sigmoid(x):
    return 1.0 / (1.0 + jnp.exp(-x))


def _silu(x):
    return x * _sigmoid(x)


def _softplus(x):
    return jnp.maximum(x, 0.0) + jnp.log(1.0 + jnp.exp(-jnp.abs(x)))


def _rmsnorm_kernel(x_ref, w_ref, o_ref):
    x = x_ref[...]
    ms = jnp.mean(x * x, axis=-1, keepdims=True)
    o_ref[...] = (x * lax.rsqrt(ms + NORM_EPS) * w_ref[...]).astype(o_ref.dtype)


def rmsnorm_bf16(x, w, tm=512):
    T, D = x.shape
    return pl.pallas_call(
        _rmsnorm_kernel,
        out_shape=jax.ShapeDtypeStruct((T, D), BF16),
        grid=(T // tm,),
        in_specs=[pl.BlockSpec((tm, D), lambda i: (i, 0)), pl.BlockSpec((1, D), lambda i: (0, 0))],
        out_specs=pl.BlockSpec((tm, D), lambda i: (i, 0)),
        compiler_params=_params("parallel"),
        name="rmsnorm_bf16",
    )(x, w.reshape(1, D))


def _mm_kernel(h_ref, w_ref, o_ref):
    o_ref[...] = jnp.dot(h_ref[...], w_ref[...], preferred_element_type=F32)


def matmul_cols(h, w, col_start, n_cols, tm, tn):
    T, K = h.shape
    off = col_start // tn
    return pl.pallas_call(
        _mm_kernel,
        out_shape=jax.ShapeDtypeStruct((T, n_cols), F32),
        grid=(T // tm, n_cols // tn),
        in_specs=[pl.BlockSpec((tm, K), lambda i, j: (i, 0)),
                  pl.BlockSpec((K, tn), lambda i, j: (0, off + j))],
        out_specs=pl.BlockSpec((tm, tn), lambda i, j: (i, j)),
        compiler_params=_params("parallel", "arbitrary"),
        name="in_proj",
    )(h, w)


def _qk_prep_kernel(q_ref, k_ref, qw_ref, kw_ref, cos_ref, sin_ref, qo_ref, ko_ref, *km_ref):
    cos = cos_ref[...]
    sin = sin_ref[...]

    def norm_rope(x, w):
        ms = jnp.mean(x * x, axis=-1, keepdims=True)
        y = x * lax.rsqrt(ms + NORM_EPS) * w
        return y * cos + pltpu.roll(y, HEAD_DIM // 2, 1) * sin

    for h in range(N_HEADS):
        sl = slice(h * HEAD_DIM, (h + 1) * HEAD_DIM)
        qo_ref[:, sl] = norm_rope(q_ref[:, sl], qw_ref[...])
        kh = norm_rope(k_ref[:, sl], kw_ref[...])
        ko_ref[:, sl] = kh
        if km_ref:
            km_ref[0][0, :, sl] = jnp.mean(kh, axis=0, keepdims=True)


def qk_prep(qk, q_w, k_w, cos, sin, with_kmean):
    T = qk.shape[0]
    tm = MOBA_BLOCK
    n_pos = cos.shape[0] // tm
    out_shape = [jax.ShapeDtypeStruct((T, WIDTH), F32), jax.ShapeDtypeStruct((T, WIDTH), F32)]
    out_specs = [pl.BlockSpec((tm, WIDTH), lambda i: (i, 0)), pl.BlockSpec((tm, WIDTH), lambda i: (i, 0))]
    if with_kmean:
        out_shape.append(jax.ShapeDtypeStruct((T // tm, 1, WIDTH), F32))
        out_specs.append(pl.BlockSpec((1, 1, WIDTH), lambda i: (i, 0, 0)))
    return pl.pallas_call(
        _qk_prep_kernel,
        out_shape=out_shape,
        grid=(T // tm,),
        in_specs=[pl.BlockSpec((tm, WIDTH), lambda i: (i, 0)),
                  pl.BlockSpec((tm, WIDTH), lambda i: (i, 1)),
                  pl.BlockSpec((1, HEAD_DIM), lambda i: (0, 0)),
                  pl.BlockSpec((1, HEAD_DIM), lambda i: (0, 0)),
                  pl.BlockSpec((tm, HEAD_DIM), lambda i: (i % n_pos, 0)),
                  pl.BlockSpec((tm, HEAD_DIM), lambda i: (i % n_pos, 0))],
        out_specs=out_specs,
        compiler_params=_params("parallel"),
        name="qk_prep",
    )(qk, qk, q_w.reshape(1, HEAD_DIM), k_w.reshape(1, HEAD_DIM), cos, sin)


def rope_tables(pos):
    half = HEAD_DIM // 2
    inv_freq = jnp.power(ROPE_THETA, -jnp.arange(half, dtype=F32) / half)
    ang = pos.astype(F32)[:, None] * inv_freq[None, :]
    cos, sin = jnp.cos(ang), jnp.sin(ang)
    return jnp.concatenate([cos, cos], -1), jnp.concatenate([-sin, sin], -1)


def _moba_prompt_kernel(q_ref, k_ref, v_ref, km_ref, o_ref, vt_ref, m_ref, l_ref, acc_ref):
    i = pl.program_id(2)
    nb = k_ref.shape[0] // MOBA_BLOCK
    scale = HEAD_DIM ** -0.5

    @pl.when(i == 0)
    def _():
        for n in range(nb):
            vt_ref[n] = v_ref[n * MOBA_BLOCK:(n + 1) * MOBA_BLOCK, :].T.astype(BF16)

    q = q_ref[...]
    qb = q.astype(BF16)
    gate = lax.dot_general(km_ref[0], q, NT_DIMS, precision=HIGHEST, preferred_element_type=F32)
    blk = lax.broadcasted_iota(jnp.int32, gate.shape, 0)
    past = blk < i

    def selected(n):
        gn = gate[n:n + 1, :]
        beats = (gate > gn) | ((gate == gn) & (blk < n))
        rank = jnp.sum(jnp.where(past & beats, 1.0, 0.0), axis=0, keepdims=True)
        return rank < MOBA_TOPK

    def scores(kb):
        return lax.dot_general(kb.astype(BF16), qb, NT_DIMS, preferred_element_type=F32) * scale

    s = scores(k_ref[pl.ds(pl.multiple_of(i * MOBA_BLOCK, MOBA_BLOCK), MOBA_BLOCK), :])
    kpos = lax.broadcasted_iota(jnp.int32, s.shape, 0)
    qpos = lax.broadcasted_iota(jnp.int32, s.shape, 1)
    s = jnp.where(kpos <= qpos, s, -jnp.inf)
    m = jnp.max(s, axis=0, keepdims=True)
    p = jnp.exp(s - m)
    m_ref[...] = m
    l_ref[...] = jnp.sum(p, axis=0, keepdims=True)
    acc_ref[...] = jnp.dot(vt_ref[i], p.astype(BF16), preferred_element_type=F32)

    for n in range(nb - 1):
        @pl.when(n < i)
        def _(n=n):
            s = jnp.where(selected(n), scores(k_ref[n * MOBA_BLOCK:(n + 1) * MOBA_BLOCK, :]), -jnp.inf)
            m_old = m_ref[...]
            m_new = jnp.maximum(m_old, jnp.max(s, axis=0, keepdims=True))
            alpha = jnp.exp(m_old - m_new)
            p = jnp.exp(s - m_new)
            m_ref[...] = m_new
            l_ref[...] = alpha * l_ref[...] + jnp.sum(p, axis=0, keepdims=True)
            acc_ref[...] = alpha * acc_ref[...] + jnp.dot(vt_ref[n], p.astype(BF16), preferred_element_type=F32)

    o_ref[...] = (acc_ref[...] * (1.0 / l_ref[...])).T


def moba_prompt(q, k, v, kmean, batch, seq):
    nb = seq // MOBA_BLOCK
    return pl.pallas_call(
        _moba_prompt_kernel,
        out_shape=jax.ShapeDtypeStruct((batch * seq, WIDTH), F32),
        grid=(batch, N_HEADS, nb),
        in_specs=[pl.BlockSpec((MOBA_BLOCK, HEAD_DIM), lambda b, h, i: (b * nb + i, h)),
                  pl.BlockSpec((seq, HEAD_DIM), lambda b, h, i: (b, h)),
                  pl.BlockSpec((seq, HEAD_DIM), lambda b, h, i: (b, h)),
                  pl.BlockSpec((1, nb, HEAD_DIM), lambda b, h, i: (b, 0, h))],
        out_specs=pl.BlockSpec((MOBA_BLOCK, HEAD_DIM), lambda b, h, i: (b * nb + i, h)),
        scratch_shapes=[pltpu.VMEM((nb, HEAD_DIM, MOBA_BLOCK), BF16),
                        pltpu.VMEM((1, MOBA_BLOCK), F32),
                        pltpu.VMEM((1, MOBA_BLOCK), F32),
                        pltpu.VMEM((HEAD_DIM, MOBA_BLOCK), F32)],
        compiler_params=_params("parallel", "parallel", "arbitrary"),
        name="moba_prompt",
    )(q, k, v, kmean)


def _moba_sample_kernel(pt_ref, q_ref, kn_ref, vn_ref, bias_ref, k0_ref, k1_ref, v0_ref, v1_ref, o_ref,
                        g_s, m_s, l_s, o_s):
    n = pl.program_id(1)
    n_past = pl.num_programs(1)
    scale = HEAD_DIM ** -0.5
    rows = q_ref.shape[1]
    n_q = rows // N_HEADS
    q = q_ref[0]
    qb = q.astype(BF16)
    bias = bias_ref[...]

    def page_scores(k_ref):
        return lax.dot_general(qb, k_ref[0].astype(BF16), NT_DIMS, preferred_element_type=F32) * scale + bias

    def page_ksum(k_ref):
        return jnp.sum(k_ref[0].reshape(PAGE_SIZE, N_HEADS, HEAD_DIM), axis=0)

    s0 = page_scores(k0_ref)
    s1 = page_scores(k1_ref)
    m = jnp.maximum(jnp.max(s0, axis=-1, keepdims=True), jnp.max(s1, axis=-1, keepdims=True))
    p0 = jnp.exp(s0 - m)
    p1 = jnp.exp(s1 - m)
    l = jnp.sum(p0, axis=-1, keepdims=True) + jnp.sum(p1, axis=-1, keepdims=True)
    o = (jnp.dot(p0.astype(BF16), v0_ref[0].astype(BF16), preferred_element_type=F32)
         + jnp.dot(p1.astype(BF16), v1_ref[0].astype(BF16), preferred_element_type=F32))
    kmean = (page_ksum(k0_ref) + page_ksum(k1_ref)) * (1.0 / MOBA_BLOCK)
    gate = jnp.sum(q * jnp.concatenate([kmean] * n_q, axis=0), axis=-1, keepdims=True)
    g_s[n] = jnp.broadcast_to(gate, (rows, LANES))
    m_s[n] = jnp.broadcast_to(m, (rows, LANES))
    l_s[n] = jnp.broadcast_to(l, (rows, LANES))
    o_s[n] = o

    @pl.when(n == n_past - 1)
    def _():
        s = lax.dot_general(qb, kn_ref[0].astype(BF16), NT_DIMS, preferred_element_type=F32) * scale
        r = lax.broadcasted_iota(jnp.int32, s.shape, 0)
        c = lax.broadcasted_iota(jnp.int32, s.shape, 1)
        ok = ((r % N_HEADS) == (c % N_HEADS)) & ((c // N_HEADS) <= (r // N_HEADS))
        s = jnp.where(ok, s, -jnp.inf)
        m_own = jnp.max(s, axis=-1, keepdims=True)
        p = jnp.exp(s - m_own)
        l_own = jnp.broadcast_to(jnp.sum(p, axis=-1, keepdims=True), (rows, LANES))
        o_own = jnp.dot(p.astype(BF16), vn_ref[0].astype(BF16), preferred_element_type=F32)
        m_own = jnp.broadcast_to(m_own, (rows, LANES))

        nb = g_s.shape[0]
        sel = []
        for a in range(nb):
            ga = g_s[a]
            rank = jnp.zeros((rows, LANES), jnp.int32)
            for b in range(nb):
                if b == a:
                    continue
                gb = g_s[b]
                beats = (gb > ga) | (gb == ga) if b < a else (gb > ga)
                rank = rank + jnp.where(beats, 1, 0)
            sel.append(rank < MOBA_TOPK)
        m_tot = m_own
        for a in range(nb):
            m_tot = jnp.where(sel[a], jnp.maximum(m_tot, m_s[a]), m_tot)
        w_own = jnp.exp(m_own - m_tot)
        l_tot = w_own * l_own
        o_tot = w_own * o_own
        for a in range(nb):
            w = jnp.where(sel[a], jnp.exp(m_s[a] - m_tot), 0.0)
            l_tot = l_tot + w * l_s[a]
            o_tot = o_tot + w * o_s[a]
        o_ref[0] = o_tot * (1.0 / l_tot)


def moba_sample(q, k_new, v_new, cache_k, cache_v, page_table):
    n_seq, rows, _ = q.shape
    n_pages = page_table.shape[1]
    n_past = n_pages * PAGE_SIZE // MOBA_BLOCK
    assert MOBA_BLOCK == 2 * PAGE_SIZE and rows == LANES
    col_head = jnp.arange(PAGE_ROWS, dtype=jnp.int32) % N_HEADS
    row_head = jnp.arange(rows, dtype=jnp.int32) % N_HEADS
    bias = jnp.where(row_head[:, None] == col_head[None, :], 0.0, -jnp.inf).astype(F32)
    seq_spec = pl.BlockSpec((1, rows, HEAD_DIM), lambda b, n, pt: (b, 0, 0))

    def page_spec(which):
        return pl.BlockSpec((1, PAGE_ROWS, HEAD_DIM), lambda b, n, pt: (pt[b * n_pages + 2 * n + which], 0, 0))

    grid_spec = pltpu.PrefetchScalarGridSpec(
        num_scalar_prefetch=1,
        grid=(n_seq, n_past),
        in_specs=[seq_spec, seq_spec, seq_spec,
                  pl.BlockSpec((rows, PAGE_ROWS), lambda b, n, pt: (0, 0)),
                  page_spec(0), page_spec(1), page_spec(0), page_spec(1)],
        out_specs=pl.BlockSpec((1, rows, HEAD_DIM), lambda b, n, pt: (b, 0, 0)),
        scratch_shapes=[pltpu.VMEM((n_past, rows, LANES), F32) for _ in range(4)],
    )
    return pl.pallas_call(
        _moba_sample_kernel,
        out_shape=jax.ShapeDtypeStruct((n_seq, rows, HEAD_DIM), F32),
        grid_spec=grid_spec,
        compiler_params=_params("parallel", "arbitrary"),
        name="moba_sample",
    )(page_table.reshape(-1), q, k_new, v_new, bias, cache_k, cache_k, cache_v, cache_v)


def _gdn_prep_kernel(x_ref, w_ref, *refs, has_halo):
    if has_halo:
        halo_ref, y_ref, c_ref = refs
    else:
        y_ref, c_ref = refs
    c = pl.program_id(1)
    x = x_ref[...]
    L = x.shape[1]
    row = lax.broadcasted_iota(jnp.int32, x.shape, 1)
    w = w_ref[...]
    acc = x * w[CONV_WIDTH - 1:CONV_WIDTH, :]
    for k in range(1, CONV_WIDTH):
        prev = pltpu.roll(x, k, 1)
        if has_halo:
            assert L == 8
            prev = jnp.where(row < k, pltpu.roll(halo_ref[...], k, 1), prev)
        else:
            prev = jnp.where(row < k, 0.0, prev)
        acc = acc + prev * w[CONV_WIDTH - 1 - k:CONV_WIDTH - k, :]
    y = _silu(acc)
    c_ref[...] = x_ref[:, L - (CONV_WIDTH - 1):, :]

    @pl.when(c < 2 * N_HEADS)
    def _():
        yn = y * lax.rsqrt(jnp.sum(y * y, axis=-1, keepdims=True) + NORM_EPS)
        y_ref[...] = yn * jnp.where(c < N_HEADS, HEAD_DIM ** -0.5, 1.0)

    @pl.when(c >= 2 * N_HEADS)
    def _():
        y_ref[...] = y


def gdn_prep(x, conv_w, halo, n_seq_blk):
    n_seq, L, _ = x.shape
    n_strips = CONV_CH // LANES
    in_specs = [pl.BlockSpec((n_seq_blk, L, LANES), lambda s, c: (s, 0, c)),
                pl.BlockSpec((CONV_WIDTH, LANES), lambda s, c: (0, c))]
    args = [x, conv_w]
    if halo is not None:
        in_specs.append(pl.BlockSpec((n_seq_blk, 8, LANES), lambda s, c: (s, 0, c)))
        args.append(halo)
    return pl.pallas_call(
        functools.partial(_gdn_prep_kernel, has_halo=halo is not None),
        out_shape=[jax.ShapeDtypeStruct((n_seq, L, CONV_CH), F32),
                   jax.ShapeDtypeStruct((n_seq, CONV_WIDTH - 1, CONV_CH), F32)],
        grid=(n_seq // n_seq_blk, n_strips),
        in_specs=in_specs,
        out_specs=[pl.BlockSpec((n_seq_blk, L, LANES), lambda s, c: (s, 0, c)),
                   pl.BlockSpec((n_seq_blk, CONV_WIDTH - 1, LANES), lambda s, c: (s, 0, c))],
        compiler_params=_params("parallel", "parallel"),
        name="gdn_prep",
    )(*args)


def _chunk_cumsum(g, chunk):
    row = lax.broadcasted_iota(jnp.int32, g.shape, 0) % chunk
    shift = 1
    while shift < chunk:
        g = g + jnp.where(row >= shift, pltpu.roll(g, shift, 0), 0.0)
        shift *= 2
    return g


def _unit_lower_inverse(a, chunk):
    def mm(x, y):
        return jnp.dot(x, y, precision=HIGHEST, preferred_element_type=F32)

    eye = (lax.broadcasted_iota(jnp.int32, a.shape, 0) == lax.broadcasted_iota(jnp.int32, a.shape, 1)).astype(F32)
    inv = eye - a
    power = a
    span = 2
    while span < chunk:
        power = mm(power, power)
        inv = inv + mm(inv, power)
        span *= 2
    return inv


def _delta_chunk(q, k, v, g_col, b_col, g_row, state, w_norm):
    C = q.shape[0]
    ii = lax.broadcasted_iota(jnp.int32, (C, C), 0)
    jj = lax.broadcasted_iota(jnp.int32, (C, C), 1)
    incl = ii >= jj
    decay = jnp.where(incl, jnp.exp(jnp.where(incl, g_col - g_row, 0.0)), 0.0)
    kb = k.astype(BF16)
    kk = lax.dot_general(kb, kb, NT_DIMS, preferred_element_type=F32)
    a = jnp.where(ii > jj, b_col * kk * decay, 0.0)
    inv = _unit_lower_inverse(a, C)
    eg = jnp.exp(g_col)
    u = jnp.dot(inv, v * b_col, precision=HIGHEST, preferred_element_type=F32)
    w = jnp.dot(inv, k * (b_col * eg), precision=HIGHEST, preferred_element_type=F32)
    qk = lax.dot_general(q.astype(BF16), kb, NT_DIMS, preferred_element_type=F32) * decay
    sb = state.astype(BF16)
    v_new = u - jnp.dot(w.astype(BF16), sb, preferred_element_type=F32)
    vb = v_new.astype(BF16)
    o = (jnp.dot((q * eg).astype(BF16), sb, preferred_element_type=F32)
         + jnp.dot(qk.astype(BF16), vb, preferred_element_type=F32))
    g_last = g_col[C - 1:C, :]
    k_dec = (k * jnp.exp(g_last - g_col)).astype(BF16)
    new_state = state * jnp.exp(g_last) + lax.dot_general(k_dec, vb, TN_DIMS, preferred_element_type=F32)
    o = o * lax.rsqrt(jnp.mean(o * o, axis=-1, keepdims=True) + NORM_EPS) * w_norm
    return o, new_state


def _gate_terms(ba_ref, alog_ref, dtb_ref, chunk):
    ba = ba_ref[...]
    beta = _sigmoid(ba)
    g = -jnp.exp(alog_ref[...]) * _softplus(ba + dtb_ref[...])
    return beta, _chunk_cumsum(g, chunk)


def _row_form(g_cum_chunk, lane):
    C = g_cum_chunk.shape[0]
    onehot = (lax.broadcasted_iota(jnp.int32, (C, LANES), 1) == lane).astype(F32)
    return lax.dot_general(onehot, g_cum_chunk, NT_DIMS, precision=HIGHEST, preferred_element_type=F32)


def _gdn_prompt_kernel(q_ref, k_ref, v_ref, ba_ref, alog_ref, dtb_ref, wn_ref, o_ref, s_out_ref, s_ref, *, chunk, hb):
    t = pl.program_id(2)
    hg = pl.program_id(1)
    rows = q_ref.shape[1]

    @pl.when(t == 0)
    def _():
        s_ref[...] = jnp.zeros_like(s_ref)

    beta, g_cum = _gate_terms(ba_ref.at[0], alog_ref, dtb_ref, chunk)
    w_norm = wn_ref[...]
    for hh in range(hb):
        sl = slice(hh * HEAD_DIM, (hh + 1) * HEAD_DIM)
        lane_b = hg * hb + hh
        sel_b = lax.broadcasted_iota(jnp.int32, (rows, LANES), 1) == lane_b
        sel_g = lax.broadcasted_iota(jnp.int32, (rows, LANES), 1) == lane_b + N_HEADS
        b_col_all = jnp.sum(jnp.where(sel_b, beta, 0.0), axis=-1, keepdims=True)
        g_col_all = jnp.sum(jnp.where(sel_g, g_cum, 0.0), axis=-1, keepdims=True)
        state = s_ref[hh]
        for c0 in range(0, rows, chunk):
            rs = slice(c0, c0 + chunk)
            g_row = _row_form(g_cum[rs, :], lane_b + N_HEADS)
            o, state = _delta_chunk(q_ref[0, rs, sl], k_ref[0, rs, sl], v_ref[0, rs, sl],
                                    g_col_all[rs, :], b_col_all[rs, :], g_row, state, w_norm)
            o_ref[0, rs, sl] = o
        s_ref[hh] = state

    @pl.when(t == pl.num_programs(2) - 1)
    def _():
        s_out_ref[0] = s_ref[...]


def gdn_prompt(qkv, ba, a_log_pad, dt_bias_pad, w_norm, chunk=64, hb=4, rows=128):
    B, L, _ = qkv.shape
    n_hg = N_HEADS // hb
    wblk = hb * HEAD_DIM
    vec = pl.BlockSpec((1, LANES), lambda b, h, t: (0, 0))
    return pl.pallas_call(
        functools.partial(_gdn_prompt_kernel, chunk=chunk, hb=hb),
        out_shape=[jax.ShapeDtypeStruct((B, L, WIDTH), F32),
                   jax.ShapeDtypeStruct((B, N_HEADS, HEAD_DIM, HEAD_DIM), F32)],
        grid=(B, n_hg, L // rows),
        in_specs=[pl.BlockSpec((1, rows, wblk), lambda b, h, t: (b, t, h)),
                  pl.BlockSpec((1, rows, wblk), lambda b, h, t: (b, t, n_hg + h)),
                  pl.BlockSpec((1, rows, wblk), lambda b, h, t: (b, t, 2 * n_hg + h)),
                  pl.BlockSpec((1, rows, LANES), lambda b, h, t: (b, t, 0)),
                  vec, vec, vec],
        out_specs=[pl.BlockSpec((1, rows, wblk), lambda b, h, t: (b, t, h)),
                   pl.BlockSpec((1, hb, HEAD_DIM, HEAD_DIM), lambda b, h, t: (b, h, 0, 0))],
        scratch_shapes=[pltpu.VMEM((hb, HEAD_DIM, HEAD_DIM), F32)],
        compiler_params=_params("parallel", "parallel", "arbitrary"),
        name="gdn_prompt",
    )(qkv, qkv, qkv, ba, a_log_pad, dt_bias_pad, w_norm.reshape(1, HEAD_DIM))


def _gdn_sample_kernel(q_ref, k_ref, v_ref, ba_ref, s_in_ref, alog_ref, dtb_ref, wn_ref, o_ref, s_out_ref,
                       beta_s, gcum_s, *, chunk, hb):
    hg = pl.program_id(1)
    n_seq = s_in_ref.shape[0]
    beta, g_cum = _gate_terms(ba_ref, alog_ref, dtb_ref, chunk)
    beta_s[...] = beta
    gcum_s[...] = g_cum
    w_norm = wn_ref[...]

    def body(s, carry):
        rs = pl.ds(pl.multiple_of(s * chunk, chunk), chunk)
        beta_c = beta_s[rs, :]
        gcum_c = gcum_s[rs, :]
        lane = lax.broadcasted_iota(jnp.int32, (chunk, LANES), 1)
        for hh in range(hb):
            sl = slice(hh * HEAD_DIM, (hh + 1) * HEAD_DIM)
            lane_b = hg * hb + hh
            b_col = jnp.sum(jnp.where(lane == lane_b, beta_c, 0.0), axis=-1, keepdims=True)
            g_col = jnp.sum(jnp.where(lane == lane_b + N_HEADS, gcum_c, 0.0), axis=-1, keepdims=True)
            g_row = _row_form(gcum_c, lane_b + N_HEADS)
            o, state = _delta_chunk(q_ref[rs, sl], k_ref[rs, sl], v_ref[rs, sl], g_col, b_col, g_row,
                                    s_in_ref[s, hh], w_norm)
            o_ref[rs, sl] = o
            s_out_ref[s, hh] = state
        return carry

    lax.fori_loop(0, n_seq, body, 0)


def gdn_sample(qkv, ba, state, a_log_pad, dt_bias_pad, w_norm, chunk, hb=4, seq_blk=16):
    T = qkv.shape[0]
    n_seq = T // chunk
    rows = seq_blk * chunk
    n_hg = N_HEADS // hb
    wblk = hb * HEAD_DIM
    vec = pl.BlockSpec((1, LANES), lambda s, h: (0, 0))
    st = pl.BlockSpec((seq_blk, hb, HEAD_DIM, HEAD_DIM), lambda s, h: (s, h, 0, 0))
    return pl.pallas_call(
        functools.partial(_gdn_sample_kernel, chunk=chunk, hb=hb),
        out_shape=[jax.ShapeDtypeStruct((T, WIDTH), F32),
                   jax.ShapeDtypeStruct(state.shape, F32)],
        grid=(n_seq // seq_blk, n_hg),
        in_specs=[pl.BlockSpec((rows, wblk), lambda s, h: (s, h)),
                  pl.BlockSpec((rows, wblk), lambda s, h: (s, n_hg + h)),
                  pl.BlockSpec((rows, wblk), lambda s, h: (s, 2 * n_hg + h)),
                  pl.BlockSpec((rows, LANES), lambda s, h: (s, 0)),
                  st, vec, vec, vec],
        out_specs=[pl.BlockSpec((rows, wblk), lambda s, h: (s, h)), st],
        scratch_shapes=[pltpu.VMEM((rows, LANES), F32), pltpu.VMEM((rows, LANES), F32)],
        compiler_params=_params("parallel", "parallel"),
        name="gdn_sample",
    )(qkv, qkv, qkv, ba, state, a_log_pad, dt_bias_pad, w_norm.reshape(1, HEAD_DIM))


def _merge_kernel(oa_ref, za_ref, od_ref, zb_ref, ga_ref, gb_ref, wa_ref, wd_ref, o_ref):
    a = (oa_ref[...] * _silu(za_ref[...])).astype(BF16)
    d = (od_ref[...] * _silu(zb_ref[...])).astype(BF16)
    pa = jnp.dot(a, wa_ref[...], preferred_element_type=F32)
    pd = jnp.dot(d, wd_ref[...], preferred_element_type=F32)
    o_ref[...] = (_sigmoid(ga_ref[...]) * pa + _sigmoid(gb_ref[...]) * pd).astype(o_ref.dtype)


def merge_branches(oa, za, od, zb, ga, gb, w_a, w_d, tm=256):
    T = oa.shape[0]
    act = pl.BlockSpec((tm, WIDTH), lambda i: (i, 0))
    wgt = pl.BlockSpec((WIDTH, D_MODEL), lambda i: (0, 0), pipeline_mode=pl.Buffered(1))
    return pl.pallas_call(
        _merge_kernel,
        out_shape=jax.ShapeDtypeStruct((T, D_MODEL), BF16),
        grid=(T // tm,),
        in_specs=[act, act, act, act, act, act, wgt, wgt],
        out_specs=pl.BlockSpec((tm, D_MODEL), lambda i: (i, 0)),
        compiler_params=_params("parallel"),
        name="merge_branches",
    )(oa, za, od, zb, ga, gb, w_a, w_d)


def _out_kernel(m_ref, x_ref, p_ref, wo_ref, wg_ref, wp_ref, nw_ref, y_ref):
    x1 = x_ref[...] + jnp.dot(m_ref[...], wo_ref[...], preferred_element_type=F32)
    ms = jnp.mean(x1 * x1, axis=-1, keepdims=True)
    hn = (x1 * lax.rsqrt(ms + NORM_EPS) * nw_ref[...]).astype(BF16)
    gate = _sigmoid(jnp.dot(hn, wg_ref[...], preferred_element_type=F32))
    emb = jnp.dot(p_ref[...].astype(BF16), wp_ref[...], preferred_element_type=F32)
    y_ref[...] = x1 + gate * emb


def output_block(merged, x, p, w_out, w_gate, w_proj, norm_w, tm=256):
    T = x.shape[0]
    const = lambda shape: pl.BlockSpec(shape, lambda i: (0, 0), pipeline_mode=pl.Buffered(1))
    return pl.pallas_call(
        _out_kernel,
        out_shape=jax.ShapeDtypeStruct((T, D_MODEL), F32),
        grid=(T // tm,),
        in_specs=[pl.BlockSpec((tm, D_MODEL), lambda i: (i, 0)),
                  pl.BlockSpec((tm, D_MODEL), lambda i: (i, 0)),
                  pl.BlockSpec((tm, PLE_DIM), lambda i: (i, 0)),
                  const((D_MODEL, D_MODEL)), const((D_MODEL, D_MODEL)), const((PLE_DIM, D_MODEL)),
                  const((1, D_MODEL))],
        out_specs=pl.BlockSpec((tm, D_MODEL), lambda i: (i, 0)),
        compiler_params=_params("parallel"),
        name="output_block",
    )(merged, x, p, w_out, w_gate, w_proj, norm_w.reshape(1, D_MODEL))


def _layer(x, p, n_seq, seq_len, pos, lw, attend, conv_halo, state):
    T = x.shape[0]
    tm = 1024
    h = rmsnorm_bf16(x, lw["norm_w"])
    w_in = lw["w_in"]
    proj = lambda start, n, tn: matmul_cols(h, w_in, start, n, tm, tn)
    qk = proj(0, 2 * WIDTH, 1024)
    va = proj(2 * WIDTH, WIDTH, 1024)
    za = proj(3 * WIDTH, WIDTH, 1024)
    conv_in = proj(4 * WIDTH, CONV_CH, 1024)
    zb = proj(4 * WIDTH + CONV_CH, WIDTH, 1024)
    ga = proj(5 * WIDTH + CONV_CH, D_MODEL, 1024)
    gb = proj(5 * WIDTH + CONV_CH + D_MODEL, D_MODEL, 1024)
    ba = proj(N_MAIN, LANES, LANES)

    cos, sin = rope_tables(pos)
    if cos.shape[0] < MOBA_BLOCK:
        reps = MOBA_BLOCK // cos.shape[0]
        cos, sin = jnp.tile(cos, (reps, 1)), jnp.tile(sin, (reps, 1))
    qa, ka, *kmean = qk_prep(qk, lw["q_norm_w"], lw["k_norm_w"], cos, sin, with_kmean=attend == "prompt")
    if attend == "prompt":
        oa = moba_prompt(qa, ka, va, kmean[0].reshape(n_seq, seq_len // MOBA_BLOCK, WIDTH), n_seq, seq_len)
    else:
        per_seq = lambda t: t.reshape(n_seq, seq_len * N_HEADS, HEAD_DIM)
        oa = moba_sample(per_seq(qa), per_seq(ka), per_seq(va), lw["cache_k"], lw["cache_v"], lw["page_table"])
        oa = oa.reshape(T, WIDTH)

    qkv, new_conv = gdn_prep(conv_in.reshape(n_seq, seq_len, CONV_CH), lw["conv_w"], conv_halo,
                             n_seq_blk=1 if conv_halo is None else 16)
    if state is None:
        od, new_state = gdn_prompt(qkv, ba.reshape(n_seq, seq_len, LANES), lw["a_log"], lw["dt_bias"], lw["dn_norm_w"])
        od = od.reshape(T, WIDTH)
    else:
        od, new_state = gdn_sample(qkv.reshape(T, CONV_CH), ba, state, lw["a_log"], lw["dt_bias"], lw["dn_norm_w"],
                                   chunk=seq_len)

    merged = merge_branches(oa, za, od, zb, ga, gb, lw["w_o_attn"], lw["w_o_delta"])
    y = output_block(merged, x, p, lw["w_out"], lw["ple_gate"], lw["ple_proj"], lw["ple_norm_w"])
    return y, ka, va, new_state, new_conv


def kernel(x_prompt, x_sample, cache_k, cache_v, state_delta, state_conv, page_table, p_prompt, p_sample,
           norm_w, w_in, q_norm_w, k_norm_w, conv_w, a_log, dt_bias, dn_norm_w, w_o_attn, w_o_delta, w_out,
           ple_proj, ple_gate, ple_norm_w):
    depth = w_in.shape[0]
    B, L, _ = x_prompt.shape
    DB, DL, _ = x_sample.shape
    yp = x_prompt.reshape(B * L, D_MODEL)
    ys = x_sample.reshape(DB * DL, D_MODEL)
    outs = [[] for _ in range(8)]
    n_pool = cache_k.shape[1]
    ba_start = 4 * WIDTH + CONV_CH + WIDTH
    for i in range(depth):
        wi = w_in[i]
        w_cat = jnp.concatenate([wi[:, :ba_start], wi[:, ba_start + 2 * N_HEADS:],
                                 wi[:, ba_start:ba_start + 2 * N_HEADS],
                                 jnp.zeros((D_MODEL, LANES - 2 * N_HEADS), wi.dtype)], axis=1).astype(BF16)
        pad = lambda v, off: jnp.zeros((1, LANES), F32).at[0, off:off + N_HEADS].set(v.astype(F32))
        lw = dict(norm_w=norm_w[i], w_in=w_cat, q_norm_w=q_norm_w[i], k_norm_w=k_norm_w[i], conv_w=conv_w[i],
                  a_log=pad(a_log[i], N_HEADS), dt_bias=pad(dt_bias[i], N_HEADS), dn_norm_w=dn_norm_w[i],
                  w_o_attn=w_o_attn[i].astype(BF16), w_o_delta=w_o_delta[i].astype(BF16),
                  w_out=w_out[i].astype(BF16), ple_proj=ple_proj[i].astype(BF16),
                  ple_gate=ple_gate[i].astype(BF16), ple_norm_w=ple_norm_w[i],
                  cache_k=cache_k[i].reshape(n_pool, PAGE_ROWS, HEAD_DIM),
                  cache_v=cache_v[i].reshape(n_pool, PAGE_ROWS, HEAD_DIM), page_table=page_table)
        yp, kp, vp, sp, cp = _layer(yp, p_prompt[i].reshape(B * L, PLE_DIM), B, L, jnp.arange(L), lw,
                                    "prompt", None, None)
        halo = jnp.pad(state_conv[i], ((0, 0), (8 - (CONV_WIDTH - 1), 0), (0, 0)))
        ys, ks, vs, ss, cs = _layer(ys, p_sample[i].reshape(DB * DL, PLE_DIM), DB, DL, PAST_LEN + jnp.arange(DL), lw,
                                    "sample", halo, state_delta[i])
        for lst, val in zip(outs, (kp.reshape(B, L, N_HEADS, HEAD_DIM), vp.reshape(B, L, N_HEADS, HEAD_DIM), sp, cp,
                                   ks.reshape(DB, DL, N_HEADS, HEAD_DIM), vs.reshape(DB, DL, N_HEADS, HEAD_DIM), ss, cs)):
            lst.append(val)
    stacked = [jnp.stack(v) for v in outs]
    return (yp.reshape(B, L, D_MODEL), ys.reshape(DB, DL, D_MODEL), *stacked)
```

```python
import functools
import math

import jax
import jax.numpy as jnp
from jax import lax
from jax.experimental import pallas as pl
from jax.experimental.pallas import tpu as pltpu

F32 = jnp.float32
BF16 = jnp.bfloat16
HIGHEST = lax.Precision.HIGHEST

D_MODEL = 2048
HEAD_DIM = 128
N_HEADS = 16
WIDTH = N_HEADS * HEAD_DIM
MOBA_BLOCK = 256
MOBA_TOPK = 3
ROPE_THETA = 10000.0
CONV_CH = 3 * WIDTH
CONV_WIDTH = 4
PLE_DIM = 256
GDN_CHUNK = 64
NORM_EPS = 1e-6
PAST_LEN = 2048
PAGE_SIZE = 128
PAGE_ROWS = PAGE_SIZE * N_HEADS
N_MAIN = 4 * WIDTH + CONV_CH + WIDTH + 2 * D_MODEL
LANES = 128
VMEM_LIMIT = 56 * 1024 * 1024

NT_DIMS = (((1,), (1,)), ((), ()))
TN_DIMS = (((0,), (0,)), ((), ()))


def _params(*sem):
    return pltpu.CompilerParams(dimension_semantics=sem, vmem_limit_bytes=VMEM_LIMIT)


def _sigmoid(x):
    return 1.0 / (1.0 + jnp.exp(-x))


def _silu(x):
    return x * _sigmoid(x)


def _softplus(x):
    return jnp.maximum(x, 0.0) + jnp.log(1.0 + jnp.exp(-jnp.abs(x)))


def _rmsnorm_kernel(x_ref, w_ref, o_ref):
    x = x_ref[...]
    ms = jnp.mean(x * x, axis=-1, keepdims=True)
    o_ref[...] = (x * lax.rsqrt(ms + NORM_EPS) * w_ref[...]).astype(o_ref.dtype)


def rmsnorm_bf16(x, w, tm=512):
    T, D = x.shape
    return pl.pallas_call(
        _rmsnorm_kernel,
        out_shape=jax.ShapeDtypeStruct((T, D), BF16),
        grid=(T // tm,),
        in_specs=[pl.BlockSpec((tm, D), lambda i: (i, 0)), pl.BlockSpec((1, D), lambda i: (0, 0))],
        out_specs=pl.BlockSpec((tm, D), lambda i: (i, 0)),
        compiler_params=_params("parallel"),
        name="rmsnorm_bf16",
    )(x, w.reshape(1, D))


def _mm_kernel(h_ref, w_ref, o_ref):
    o_ref[...] = jnp.dot(h_ref[...], w_ref[...], preferred_element_type=F32)


def matmul_cols(h, w, col_start, n_cols, tm, tn):
    T, K = h.shape
    off = col_start // tn
    return pl.pallas_call(
        _mm_kernel,
        out_shape=jax.ShapeDtypeStruct((T, n_cols), F32),
        grid=(T // tm, n_cols // tn),
        in_specs=[pl.BlockSpec((tm, K), lambda i, j: (i, 0)),
                  pl.BlockSpec((K, tn), lambda i, j: (0, off + j))],
        out_specs=pl.BlockSpec((tm, tn), lambda i, j: (i, j)),
        compiler_params=_params("parallel", "arbitrary"),
        name="in_proj",
    )(h, w)


def _qk_prep_kernel(q_ref, k_ref, qw_ref, kw_ref, cos_ref, sin_ref, qo_ref, ko_ref, *km_ref):
    cos = cos_ref[...]
    sin = sin_ref[...]

    def norm_rope(x, w):
        ms = jnp.mean(x * x, axis=-1, keepdims=True)
        y = x * lax.rsqrt(ms + NORM_EPS) * w
        return y * cos + pltpu.roll(y, HEAD_DIM // 2, 1) * sin

    for h in range(N_HEADS):
        sl = slice(h * HEAD_DIM, (h + 1) * HEAD_DIM)
        qo_ref[:, sl] = norm_rope(q_ref[:, sl], qw_ref[...])
        kh = norm_rope(k_ref[:, sl], kw_ref[...])
        ko_ref[:, sl] = kh
        if km_ref:
            km_ref[0][0, :, sl] = jnp.mean(kh, axis=0, keepdims=True)


def qk_prep(qk, q_w, k_w, cos, sin, with_kmean):
    T = qk.shape[0]
    tm = MOBA_BLOCK
    n_pos = cos.shape[0] // tm
    out_shape = [jax.ShapeDtypeStruct((T, WIDTH), F32), jax.ShapeDtypeStruct((T, WIDTH), F32)]
    out_specs = [pl.BlockSpec((tm, WIDTH), lambda i: (i, 0)), pl.BlockSpec((tm, WIDTH), lambda i: (i, 0))]
    if with_kmean:
        out_shape.append(jax.ShapeDtypeStruct((T // tm, 1, WIDTH), F32))
        out_specs.append(pl.BlockSpec((1, 1, WIDTH), lambda i: (i, 0, 0)))
    return pl.pallas_call(
        _qk_prep_kernel,
        out_shape=out_shape,
        grid=(T // tm,),
        in_specs=[pl.BlockSpec((tm, WIDTH), lambda i: (i, 0)),
                  pl.BlockSpec((tm, WIDTH), lambda i: (i, 1)),
                  pl.BlockSpec((1, HEAD_DIM), lambda i: (0, 0)),
                  pl.BlockSpec((1, HEAD_DIM), lambda i: (0, 0)),
                  pl.BlockSpec((tm, HEAD_DIM), lambda i: (i % n_pos, 0)),
                  pl.BlockSpec((tm, HEAD_DIM), lambda i: (i % n_pos, 0))],
        out_specs=out_specs,
        compiler_params=_params("parallel"),
        name="qk_prep",
    )(qk, qk, q_w.reshape(1, HEAD_DIM), k_w.reshape(1, HEAD_DIM), cos, sin)


def rope_tables(pos):
    half = HEAD_DIM // 2
    inv_freq = jnp.power(ROPE_THETA, -jnp.arange(half, dtype=F32) / half)
    ang = pos.astype(F32)[:, None] * inv_freq[None, :]
    cos, sin = jnp.cos(ang), jnp.sin(ang)
    return jnp.concatenate([cos, cos], -1), jnp.concatenate([-sin, sin], -1)


LOG2E = 1.4426950408889634


def _topk_block_mask(gate, n_past):
    blk = lax.broadcasted_iota(jnp.int32, gate.shape, 0)
    cand = blk < n_past
    rows = []
    for n in range(n_past):
        gn = gate[n:n + 1, :]
        beats = (gate > gn) | ((gate == gn) & (blk < n))
        rank = jnp.sum(jnp.where(cand & beats, 1.0, 0.0), axis=0, keepdims=True)
        rows.append(jnp.where(rank < MOBA_TOPK, 1.0, 0.0))
    rows += [jnp.zeros_like(rows[0])] * (gate.shape[0] - n_past)
    return jnp.concatenate(rows, axis=0)


def _moba_prompt_kernel(q_ref, k_ref, v_ref, km_ref, o_ref):
    nb = k_ref.shape[0] // MOBA_BLOCK
    kb = k_ref[...].astype(BF16)
    vb = v_ref[...].astype(BF16)
    km = km_ref[0]
    rpos = lax.broadcasted_iota(jnp.int32, (MOBA_BLOCK, MOBA_BLOCK), 0)
    cpos = lax.broadcasted_iota(jnp.int32, (MOBA_BLOCK, MOBA_BLOCK), 1)
    causal = cpos <= rpos
    for c in range(nb):
        rs = slice(c * MOBA_BLOCK, (c + 1) * MOBA_BLOCK)
        n_keys = (c + 1) * MOBA_BLOCK
        q = q_ref[rs, :]
        qb = (q * (HEAD_DIM ** -0.5 * LOG2E)).astype(BF16)
        s = lax.dot_general(qb, kb[:n_keys, :], NT_DIMS, preferred_element_type=F32)
        parts = []
        if c > MOBA_TOPK:
            gate = lax.dot_general(km, q, NT_DIMS, precision=HIGHEST, preferred_element_type=F32)
            keep = _topk_block_mask(gate, c)
            keep = jnp.concatenate([keep, jnp.zeros((LANES - keep.shape[0], MOBA_BLOCK), F32)], axis=0).T
            for n in range(c):
                parts.append(jnp.where(keep[:, n:n + 1] > 0.5, s[:, n * MOBA_BLOCK:(n + 1) * MOBA_BLOCK], -jnp.inf))
        else:
            parts = [s[:, n * MOBA_BLOCK:(n + 1) * MOBA_BLOCK] for n in range(c)]
        parts.append(jnp.where(causal, s[:, c * MOBA_BLOCK:], -jnp.inf))
        s = jnp.concatenate(parts, axis=1) if c else parts[0]
        m = jnp.max(s, axis=-1, keepdims=True)
        p = jnp.exp2(s - m)
        l = jnp.sum(p, axis=-1, keepdims=True)
        o = jnp.dot(p.astype(BF16), vb[:n_keys, :], preferred_element_type=F32)
        o_ref[rs, :] = o * (1.0 / l)


def moba_prompt(q, k, v, kmean, batch, seq):
    nb = seq // MOBA_BLOCK
    assert nb <= 8
    strip = pl.BlockSpec((seq, HEAD_DIM), lambda b, h: (b, h))
    return pl.pallas_call(
        _moba_prompt_kernel,
        out_shape=jax.ShapeDtypeStruct((batch * seq, WIDTH), F32),
        grid=(batch, N_HEADS),
        in_specs=[strip, strip, strip, pl.BlockSpec((1, nb, HEAD_DIM), lambda b, h: (b, 0, h))],
        out_specs=strip,
        compiler_params=_params("parallel", "parallel"),
        name="moba_prompt",
    )(q, k, v, kmean)


def _moba_sample_kernel(pt_ref, q_ref, kn_ref, vn_ref, k0_ref, k1_ref, v0_ref, v1_ref, o_ref, g_s, m_s, l_s, o_s):
    n = pl.program_id(1)
    n_past = pl.num_programs(1)
    n_q = q_ref.shape[0]
    qs = q_ref[...]

    def head_rows(ref, h):
        return ref[0, pl.ds(h, PAGE_SIZE, stride=N_HEADS), :]

    heads = range(N_HEADS)
    sls = [slice(h * HEAD_DIM, (h + 1) * HEAD_DIM) for h in heads]
    qb = [(qs[:, sl] * (HEAD_DIM ** -0.5 * LOG2E)).astype(BF16) for sl in sls]
    k0 = [head_rows(k0_ref, h) for h in heads]
    k1 = [head_rows(k1_ref, h) for h in heads]
    s0 = [lax.dot_general(qb[h], k0[h].astype(BF16), NT_DIMS, preferred_element_type=F32) for h in heads]
    s1 = [lax.dot_general(qb[h], k1[h].astype(BF16), NT_DIMS, preferred_element_type=F32) for h in heads]
    m = [jnp.max(jnp.maximum(s0[h], s1[h]), axis=-1, keepdims=True) for h in heads]
    p0 = [jnp.exp2(s0[h] - m[h]) for h in heads]
    p1 = [jnp.exp2(s1[h] - m[h]) for h in heads]
    o0 = [jnp.dot(p0[h].astype(BF16), head_rows(v0_ref, h).astype(BF16), preferred_element_type=F32) for h in heads]
    o1 = [jnp.dot(p1[h].astype(BF16), head_rows(v1_ref, h).astype(BF16), preferred_element_type=F32) for h in heads]
    for h in heads:
        kmean = jnp.sum(k0[h] + k1[h], axis=0, keepdims=True) * (1.0 / MOBA_BLOCK)
        gate = jnp.sum(qs[:, sls[h]] * kmean, axis=-1, keepdims=True)
        l = jnp.sum(p0[h] + p1[h], axis=-1, keepdims=True)
        g_s[n, h] = jnp.broadcast_to(gate, (n_q, LANES))
        m_s[n, h] = jnp.broadcast_to(m[h], (n_q, LANES))
        l_s[n, h] = jnp.broadcast_to(l, (n_q, LANES))
        o_s[n, h] = o0[h] + o1[h]

    @pl.when(n == n_past - 1)
    def _():
        nb = g_s.shape[0]
        r = lax.broadcasted_iota(jnp.int32, (n_q, LANES), 0)
        c = lax.broadcasted_iota(jnp.int32, (n_q, LANES), 1)
        own_ok = c <= r
        pad = jnp.zeros((LANES - n_q, HEAD_DIM), F32)
        kn = [jnp.concatenate([kn_ref[:, sl], pad], axis=0).astype(BF16) for sl in sls]
        vn = [jnp.concatenate([vn_ref[:, sl], pad], axis=0).astype(BF16) for sl in sls]
        s_own = [jnp.where(own_ok, lax.dot_general(qb[h], kn[h], NT_DIMS, preferred_element_type=F32), -jnp.inf)
                 for h in heads]
        m_owns = [jnp.max(s, axis=-1, keepdims=True) for s in s_own]
        p_own = [jnp.exp2(s_own[h] - m_owns[h]) for h in heads]
        o_owns = [jnp.dot(p_own[h].astype(BF16), vn[h], preferred_element_type=F32) for h in heads]
        for h in heads:
            sl = sls[h]
            m_own = m_owns[h]
            l_own = jnp.sum(p_own[h], axis=-1, keepdims=True)
            o_own = o_owns[h]
            keep = []
            for a in range(nb):
                ga = g_s[a, h]
                rank = jnp.zeros((n_q, LANES), F32)
                for b in range(nb):
                    if b != a:
                        gb = g_s[b, h]
                        rank = rank + jnp.where((gb >= ga) if b < a else (gb > ga), 1.0, 0.0)
                keep.append(rank < MOBA_TOPK)
            m_tot = jnp.broadcast_to(m_own, (n_q, LANES))
            for a in range(nb):
                m_tot = jnp.where(keep[a], jnp.maximum(m_tot, m_s[a, h]), m_tot)
            w_own = jnp.exp2(m_own - m_tot)
            l_tot = w_own * l_own
            o_tot = w_own * o_own
            for a in range(nb):
                w = jnp.where(keep[a], jnp.exp2(m_s[a, h] - m_tot), 0.0)
                l_tot = l_tot + w * l_s[a, h]
                o_tot = o_tot + w * o_s[a, h]
            o_ref[:, sl] = o_tot * (1.0 / l_tot)


def moba_sample(q, k_new, v_new, cache_k, cache_v, page_table, n_q):
    n_seq, n_pages = page_table.shape
    n_past = n_pages * PAGE_SIZE // MOBA_BLOCK
    assert MOBA_BLOCK == 2 * PAGE_SIZE and n_q == 8
    seq_spec = pl.BlockSpec((n_q, WIDTH), lambda b, n, pt: (b, 0))

    def page_spec(which):
        return pl.BlockSpec((1, PAGE_ROWS, HEAD_DIM), lambda b, n, pt: (pt[b * n_pages + 2 * n + which], 0, 0))

    grid_spec = pltpu.PrefetchScalarGridSpec(
        num_scalar_prefetch=1,
        grid=(n_seq, n_past),
        in_specs=[seq_spec, seq_spec, seq_spec, page_spec(0), page_spec(1), page_spec(0), page_spec(1)],
        out_specs=seq_spec,
        scratch_shapes=[pltpu.VMEM((n_past, N_HEADS, n_q, LANES), F32) for _ in range(4)],
    )
    return pl.pallas_call(
        _moba_sample_kernel,
        out_shape=jax.ShapeDtypeStruct((n_seq * n_q, WIDTH), F32),
        grid_spec=grid_spec,
        compiler_params=_params("parallel", "arbitrary"),
        name="moba_sample",
    )(page_table.reshape(-1), q, k_new, v_new, cache_k, cache_k, cache_v, cache_v)


def _gdn_prep_kernel(x_ref, w_ref, *refs, has_halo):
    if has_halo:
        halo_ref, y_ref, c_ref = refs
    else:
        y_ref, c_ref = refs
    c = pl.program_id(1)
    x = x_ref[...]
    L = x.shape[1]
    row = lax.broadcasted_iota(jnp.int32, x.shape, 1)
    w = w_ref[...]
    acc = x * w[CONV_WIDTH - 1:CONV_WIDTH, :]
    for k in range(1, CONV_WIDTH):
        prev = pltpu.roll(x, k, 1)
        if has_halo:
            assert L == 8
            prev = jnp.where(row < k, pltpu.roll(halo_ref[...], k, 1), prev)
        else:
            prev = jnp.where(row < k, 0.0, prev)
        acc = acc + prev * w[CONV_WIDTH - 1 - k:CONV_WIDTH - k, :]
    y = _silu(acc)
    c_ref[...] = x_ref[:, L - (CONV_WIDTH - 1):, :]

    @pl.when(c < 2 * N_HEADS)
    def _():
        yn = y * lax.rsqrt(jnp.sum(y * y, axis=-1, keepdims=True) + NORM_EPS)
        y_ref[...] = yn * jnp.where(c < N_HEADS, HEAD_DIM ** -0.5, 1.0)

    @pl.when(c >= 2 * N_HEADS)
    def _():
        y_ref[...] = y


def gdn_prep(x, conv_w, halo, n_seq_blk):
    n_seq, L, _ = x.shape
    n_strips = CONV_CH // LANES
    in_specs = [pl.BlockSpec((n_seq_blk, L, LANES), lambda s, c: (s, 0, c)),
                pl.BlockSpec((CONV_WIDTH, LANES), lambda s, c: (0, c))]
    args = [x, conv_w]
    if halo is not None:
        in_specs.append(pl.BlockSpec((n_seq_blk, 8, LANES), lambda s, c: (s, 0, c)))
        args.append(halo)
    return pl.pallas_call(
        functools.partial(_gdn_prep_kernel, has_halo=halo is not None),
        out_shape=[jax.ShapeDtypeStruct((n_seq, L, CONV_CH), F32),
                   jax.ShapeDtypeStruct((n_seq, CONV_WIDTH - 1, CONV_CH), F32)],
        grid=(n_seq // n_seq_blk, n_strips),
        in_specs=in_specs,
        out_specs=[pl.BlockSpec((n_seq_blk, L, LANES), lambda s, c: (s, 0, c)),
                   pl.BlockSpec((n_seq_blk, CONV_WIDTH - 1, LANES), lambda s, c: (s, 0, c))],
        compiler_params=_params("parallel", "parallel"),
        name="gdn_prep",
    )(*args)


def _chunk_cumsum(g, chunk):
    row = lax.broadcasted_iota(jnp.int32, g.shape, 0) % chunk
    shift = 1
    while shift < chunk:
        g = g + jnp.where(row >= shift, pltpu.roll(g, shift, 0), 0.0)
        shift *= 2
    return g


def _gate_terms(ba_ref, alog_ref, dtb_ref, chunk):
    ba = ba_ref[...]
    beta = _sigmoid(ba)
    g = -jnp.exp(alog_ref[...]) * _softplus(ba + dtb_ref[...])
    return beta, _chunk_cumsum(g, chunk)


def _bdot(x, y):
    return jnp.dot(x.astype(BF16), y.astype(BF16), preferred_element_type=F32)


def _gdn_stage1_kernel(q_ref, k_ref, v_ref, ba_ref, alog_ref, dtb_ref,
                       u_ref, w_ref, qe_ref, kdt_ref, qk_ref, gt_ref, *, chunk, hb):
    hg = pl.program_id(1)
    rows = q_ref.shape[0]
    beta, g_cum = _gate_terms(ba_ref, alog_ref, dtb_ref, chunk)
    gt_ref[...] = g_cum.T
    ii = lax.broadcasted_iota(jnp.int32, (rows, rows), 0)
    jj = lax.broadcasted_iota(jnp.int32, (rows, rows), 1)
    same = (ii // chunk) == (jj // chunk)
    incl = same & (ii >= jj)
    strict = same & (ii > jj)
    is_last = jj == (ii // chunk) * chunk + (chunk - 1)
    lane = lax.broadcasted_iota(jnp.int32, (rows, LANES), 1)
    heads = range(hb)
    sls = [slice(hh * HEAD_DIM, (hh + 1) * HEAD_DIM) for hh in heads]
    b_col, g_col, gl_col, decay = [], [], [], []
    for hh in heads:
        head = hg * hb + hh
        b_col.append(jnp.sum(jnp.where(lane == head, beta, 0.0), axis=-1, keepdims=True))
        g_col.append(jnp.sum(jnp.where(lane == head + N_HEADS, g_cum, 0.0), axis=-1, keepdims=True))
        g_row = gt_ref[pl.ds(head + N_HEADS, 1), :]
        gl_col.append(jnp.sum(jnp.where(is_last, g_row, 0.0), axis=-1, keepdims=True))
        decay.append(jnp.where(incl, jnp.exp(jnp.where(incl, g_col[hh] - g_row, 0.0)), 0.0))
    kb = [k_ref[:, sl].astype(BF16) for sl in sls]
    kk = [lax.dot_general(kb[hh], kb[hh], NT_DIMS, preferred_element_type=F32) for hh in heads]
    power = [jnp.where(strict, b_col[hh] * kk[hh] * decay[hh], 0.0) for hh in heads]
    n = [-a for a in power]
    span = 2
    while span < chunk:
        power = [_bdot(p, p) for p in power]
        n = [n[hh] + power[hh] + _bdot(n[hh], power[hh]) for hh in heads]
        span *= 2
    eg = [jnp.exp(g) for g in g_col]
    rv = [v_ref[:, sls[hh]] * b_col[hh] for hh in heads]
    rk = [k_ref[:, sls[hh]] * (b_col[hh] * eg[hh]) for hh in heads]
    nv = [_bdot(n[hh], rv[hh]) for hh in heads]
    nk = [_bdot(n[hh], rk[hh]) for hh in heads]
    qk = [lax.dot_general(q_ref[:, sls[hh]].astype(BF16), kb[hh], NT_DIMS, preferred_element_type=F32) for hh in heads]
    for hh in heads:
        sl = sls[hh]
        u_ref[:, sl] = rv[hh] + nv[hh]
        w_ref[:, sl] = (rk[hh] + nk[hh]).astype(w_ref.dtype)
        qe_ref[:, sl] = (q_ref[:, sl] * eg[hh]).astype(qe_ref.dtype)
        qk_ref[0, sl, :] = (qk[hh] * decay[hh]).astype(qk_ref.dtype)
        kdt_ref[0, sl, :] = (k_ref[:, sl] * jnp.exp(gl_col[hh] - g_col[hh])).T.astype(kdt_ref.dtype)


def gdn_stage1(qkv, ba, a_log_pad, dt_bias_pad, chunk, mid_dtype, hb=8):
    T = qkv.shape[0]
    rows = LANES
    n_hg = N_HEADS // hb
    wblk = hb * HEAD_DIM
    vec = pl.BlockSpec((1, LANES), lambda i, h: (0, 0))
    tok = pl.BlockSpec((rows, wblk), lambda i, h: (i, h))
    blk = pl.BlockSpec((1, wblk, rows), lambda i, h: (i, h, 0))
    return pl.pallas_call(
        functools.partial(_gdn_stage1_kernel, chunk=chunk, hb=hb),
        out_shape=[jax.ShapeDtypeStruct((T, WIDTH), F32),
                   jax.ShapeDtypeStruct((T, WIDTH), mid_dtype),
                   jax.ShapeDtypeStruct((T, WIDTH), mid_dtype),
                   jax.ShapeDtypeStruct((T // rows, WIDTH, rows), mid_dtype),
                   jax.ShapeDtypeStruct((T // rows, WIDTH, rows), mid_dtype)],
        grid=(T // rows, n_hg),
        in_specs=[tok,
                  pl.BlockSpec((rows, wblk), lambda i, h: (i, n_hg + h)),
                  pl.BlockSpec((rows, wblk), lambda i, h: (i, 2 * n_hg + h)),
                  pl.BlockSpec((rows, LANES), lambda i, h: (i, 0)),
                  vec, vec],
        out_specs=[tok, tok, tok, blk, blk],
        scratch_shapes=[pltpu.VMEM((rows, rows), F32)],
        compiler_params=_params("parallel", "parallel"),
        name="gdn_stage1",
    )(qkv, qkv, qkv, ba, a_log_pad, dt_bias_pad)


def _delta_apply(u, w, qe, qk_rows, kdt, v_place, state, decay_last, w_norm):
    n = range(len(state))
    sb = [s.astype(BF16) for s in state]
    ws = [_bdot(w[i], sb[i]) for i in n]
    qs = [_bdot(qe[i], sb[i]) for i in n]
    v_pad = [v_place(u[i] - ws[i]) for i in n]
    o = [qs[i] + _bdot(qk_rows[i], v_pad[i]) for i in n]
    new_state = [state[i] * decay_last[i] + _bdot(kdt[i], v_pad[i]) for i in n]
    o = [x * lax.rsqrt(jnp.mean(x * x, axis=-1, keepdims=True) + NORM_EPS) * w_norm for x in o]
    return o, new_state


def _gdn_carry_kernel(u_ref, w_ref, qe_ref, kdt_ref, qk_ref, ba_ref, alog_ref, dtb_ref, wn_ref,
                      o_ref, s_out_ref, s_ref, *, chunk):
    t = pl.program_id(1)
    rows = u_ref.shape[0]
    n_chunks = rows // chunk

    @pl.when(t == 0)
    def _():
        s_ref[...] = jnp.zeros_like(s_ref)

    _, g_cum = _gate_terms(ba_ref, alog_ref, dtb_ref, chunk)
    w_norm = wn_ref[...]
    zeros = jnp.zeros((chunk, HEAD_DIM), BF16)
    heads = range(N_HEADS)
    sls = [slice(h * HEAD_DIM, (h + 1) * HEAD_DIM) for h in heads]
    state = [s_ref[h] for h in heads]
    for c in range(n_chunks):
        rs = slice(c * chunk, (c + 1) * chunk)
        last = (c + 1) * chunk - 1

        def v_place(v_new, c=c):
            return jnp.concatenate([zeros] * c + [v_new.astype(BF16)] + [zeros] * (n_chunks - 1 - c), axis=0)

        o, state = _delta_apply(
            [u_ref[rs, sl] for sl in sls], [w_ref[rs, sl] for sl in sls], [qe_ref[rs, sl] for sl in sls],
            [qk_ref[0, h * HEAD_DIM + c * chunk:h * HEAD_DIM + (c + 1) * chunk, :] for h in heads],
            [kdt_ref[0, sl, :] for sl in sls], v_place, state,
            [jnp.exp(g_cum[last:last + 1, N_HEADS + h:N_HEADS + h + 1]) for h in heads], w_norm)
        for h in heads:
            o_ref[rs, sls[h]] = o[h]
    for h in heads:
        s_ref[h] = state[h]

    @pl.when(t == pl.num_programs(1) - 1)
    def _():
        s_out_ref[0] = s_ref[...]


def gdn_carry(u, w, qe, kdt, qk, ba, a_log_pad, dt_bias_pad, w_norm, n_seq, chunk):
    T = u.shape[0]
    rows = LANES
    nblk = T // n_seq // rows
    vec = pl.BlockSpec((1, LANES), lambda b, t: (0, 0))
    tok = pl.BlockSpec((rows, WIDTH), lambda b, t: (b * nblk + t, 0))
    blk = pl.BlockSpec((1, WIDTH, rows), lambda b, t: (b * nblk + t, 0, 0))
    return pl.pallas_call(
        functools.partial(_gdn_carry_kernel, chunk=chunk),
        out_shape=[jax.ShapeDtypeStruct((T, WIDTH), F32),
                   jax.ShapeDtypeStruct((n_seq, N_HEADS, HEAD_DIM, HEAD_DIM), F32)],
        grid=(n_seq, nblk),
        in_specs=[tok, tok, tok, blk, blk,
                  pl.BlockSpec((rows, LANES), lambda b, t: (b * nblk + t, 0)), vec, vec, vec],
        out_specs=[tok, pl.BlockSpec((1, N_HEADS, HEAD_DIM, HEAD_DIM), lambda b, t: (b, 0, 0, 0))],
        scratch_shapes=[pltpu.VMEM((N_HEADS, HEAD_DIM, HEAD_DIM), F32)],
        compiler_params=_params("parallel", "arbitrary"),
        name="gdn_carry",
    )(u, w, qe, kdt, qk, ba, a_log_pad, dt_bias_pad, w_norm.reshape(1, HEAD_DIM))


def _gdn_states_kernel(u_ref, w_ref, qe_ref, kdt_ref, qk_ref, ba_ref, s_in_ref, alog_ref, dtb_ref, wn_ref,
                       o_ref, s_out_ref, gcum_s, *, chunk, hb):
    hg = pl.program_id(1)
    rows = u_ref.shape[0]
    n_seq = rows // chunk
    _, g_cum = _gate_terms(ba_ref, alog_ref, dtb_ref, chunk)
    gcum_s[...] = g_cum
    w_norm = wn_ref[...]
    row_seq = lax.broadcasted_iota(jnp.int32, (rows, HEAD_DIM), 0) // chunk
    lane = lax.broadcasted_iota(jnp.int32, (1, LANES), 1)

    heads = range(hb)
    sls = [slice(hh * HEAD_DIM, (hh + 1) * HEAD_DIM) for hh in heads]

    def body(s, carry):
        rs = pl.ds(pl.multiple_of(s * chunk, chunk), chunk)
        g_last_row = gcum_s[pl.ds(s * chunk + (chunk - 1), 1), :]

        def v_place(v_new):
            tiled = jnp.concatenate([v_new] * n_seq, axis=0)
            return jnp.where(row_seq == s, tiled, 0.0).astype(BF16)

        o, state = _delta_apply(
            [u_ref[rs, sl] for sl in sls], [w_ref[rs, sl] for sl in sls], [qe_ref[rs, sl] for sl in sls],
            [qk_ref[0, pl.ds(pl.multiple_of(hh * HEAD_DIM + s * chunk, chunk), chunk), :] for hh in heads],
            [kdt_ref[0, sl, :] for sl in sls], v_place, [s_in_ref[s, hh] for hh in heads],
            [jnp.exp(jnp.sum(jnp.where(lane == hg * hb + hh + N_HEADS, g_last_row, 0.0), axis=-1, keepdims=True))
             for hh in heads], w_norm)
        for hh in heads:
            o_ref[rs, sls[hh]] = o[hh]
            s_out_ref[s, hh] = state[hh]
        return carry

    lax.fori_loop(0, n_seq, body, 0)


def gdn_states(u, w, qe, kdt, qk, ba, state, a_log_pad, dt_bias_pad, w_norm, chunk, hb=8):
    T = u.shape[0]
    rows = LANES
    seq_blk = rows // chunk
    n_hg = N_HEADS // hb
    wblk = hb * HEAD_DIM
    vec = pl.BlockSpec((1, LANES), lambda i, h: (0, 0))
    tok = pl.BlockSpec((rows, wblk), lambda i, h: (i, h))
    blk = pl.BlockSpec((1, wblk, rows), lambda i, h: (i, h, 0))
    st = pl.BlockSpec((seq_blk, hb, HEAD_DIM, HEAD_DIM), lambda i, h: (i, h, 0, 0))
    return pl.pallas_call(
        functools.partial(_gdn_states_kernel, chunk=chunk, hb=hb),
        out_shape=[jax.ShapeDtypeStruct((T, WIDTH), F32), jax.ShapeDtypeStruct(state.shape, F32)],
        grid=(T // rows, n_hg),
        in_specs=[tok, tok, tok, blk, blk, pl.BlockSpec((rows, LANES), lambda i, h: (i, 0)), st, vec, vec, vec],
        out_specs=[tok, st],
        scratch_shapes=[pltpu.VMEM((rows, LANES), F32)],
        compiler_params=_params("parallel", "parallel"),
        name="gdn_states",
    )(u, w, qe, kdt, qk, ba, state, a_log_pad, dt_bias_pad, w_norm.reshape(1, HEAD_DIM))


def _merge_kernel(oa_ref, za_ref, od_ref, zb_ref, ga_ref, gb_ref, wa_ref, wd_ref, o_ref):
    a = (oa_ref[...] * _silu(za_ref[...])).astype(BF16)
    d = (od_ref[...] * _silu(zb_ref[...])).astype(BF16)
    pa = jnp.dot(a, wa_ref[...], preferred_element_type=F32)
    pd = jnp.dot(d, wd_ref[...], preferred_element_type=F32)
    o_ref[...] = (_sigmoid(ga_ref[...]) * pa + _sigmoid(gb_ref[...]) * pd).astype(o_ref.dtype)


def merge_branches(oa, za, od, zb, ga, gb, w_a, w_d, tm=256):
    T = oa.shape[0]
    act = pl.BlockSpec((tm, WIDTH), lambda i: (i, 0))
    wgt = pl.BlockSpec((WIDTH, D_MODEL), lambda i: (0, 0), pipeline_mode=pl.Buffered(1))
    return pl.pallas_call(
        _merge_kernel,
        out_shape=jax.ShapeDtypeStruct((T, D_MODEL), BF16),
        grid=(T // tm,),
        in_specs=[act, act, act, act, act, act, wgt, wgt],
        out_specs=pl.BlockSpec((tm, D_MODEL), lambda i: (i, 0)),
        compiler_params=_params("parallel"),
        name="merge_branches",
    )(oa, za, od, zb, ga, gb, w_a, w_d)


def _out_kernel(m_ref, x_ref, p_ref, wo_ref, wg_ref, wp_ref, nw_ref, y_ref):
    x1 = x_ref[...] + jnp.dot(m_ref[...], wo_ref[...], preferred_element_type=F32)
    ms = jnp.mean(x1 * x1, axis=-1, keepdims=True)
    hn = (x1 * lax.rsqrt(ms + NORM_EPS) * nw_ref[...]).astype(BF16)
    gate = _sigmoid(jnp.dot(hn, wg_ref[...], preferred_element_type=F32))
    emb = jnp.dot(p_ref[...].astype(BF16), wp_ref[...], preferred_element_type=F32)
    y_ref[...] = x1 + gate * emb


def output_block(merged, x, p, w_out, w_gate, w_proj, norm_w, tm=256):
    T = x.shape[0]
    const = lambda shape: pl.BlockSpec(shape, lambda i: (0, 0), pipeline_mode=pl.Buffered(1))
    return pl.pallas_call(
        _out_kernel,
        out_shape=jax.ShapeDtypeStruct((T, D_MODEL), F32),
        grid=(T // tm,),
        in_specs=[pl.BlockSpec((tm, D_MODEL), lambda i: (i, 0)),
                  pl.BlockSpec((tm, D_MODEL), lambda i: (i, 0)),
                  pl.BlockSpec((tm, PLE_DIM), lambda i: (i, 0)),
                  const((D_MODEL, D_MODEL)), const((D_MODEL, D_MODEL)), const((PLE_DIM, D_MODEL)),
                  const((1, D_MODEL))],
        out_specs=pl.BlockSpec((tm, D_MODEL), lambda i: (i, 0)),
        compiler_params=_params("parallel"),
        name="output_block",
    )(merged, x, p, w_out, w_gate, w_proj, norm_w.reshape(1, D_MODEL))


def _layer(x, p, n_seq, seq_len, pos, lw, attend, conv_halo, state):
    T = x.shape[0]
    tm = 1024
    h = rmsnorm_bf16(x, lw["norm_w"])
    w_in = lw["w_in"]
    proj = lambda start, n, tn: matmul_cols(h, w_in, start, n, tm, tn)
    qk = proj(0, 2 * WIDTH, 1024)
    va = proj(2 * WIDTH, WIDTH, 1024)
    za = proj(3 * WIDTH, WIDTH, 1024)
    conv_in = proj(4 * WIDTH, CONV_CH, 1024)
    zb = proj(4 * WIDTH + CONV_CH, WIDTH, 1024)
    ga = proj(5 * WIDTH + CONV_CH, D_MODEL, 1024)
    gb = proj(5 * WIDTH + CONV_CH + D_MODEL, D_MODEL, 1024)
    ba = proj(N_MAIN, LANES, LANES)

    cos, sin = rope_tables(pos)
    if cos.shape[0] < MOBA_BLOCK:
        reps = MOBA_BLOCK // cos.shape[0]
        cos, sin = jnp.tile(cos, (reps, 1)), jnp.tile(sin, (reps, 1))
    qa, ka, *kmean = qk_prep(qk, lw["q_norm_w"], lw["k_norm_w"], cos, sin, with_kmean=attend == "prompt")
    if attend == "prompt":
        oa = moba_prompt(qa, ka, va, kmean[0].reshape(n_seq, seq_len // MOBA_BLOCK, WIDTH), n_seq, seq_len)
    else:
        oa = moba_sample(qa, ka, va, lw["cache_k"], lw["cache_v"], lw["page_table"], seq_len)

    qkv, new_conv = gdn_prep(conv_in.reshape(n_seq, seq_len, CONV_CH), lw["conv_w"], conv_halo,
                             n_seq_blk=1 if conv_halo is None else 16)
    qkv = qkv.reshape(T, CONV_CH)
    if state is None:
        stage1 = gdn_stage1(qkv, ba, lw["a_log"], lw["dt_bias"], GDN_CHUNK, BF16)
        od, new_state = gdn_carry(*stage1, ba, lw["a_log"], lw["dt_bias"], lw["dn_norm_w"], n_seq, GDN_CHUNK)
    else:
        stage1 = gdn_stage1(qkv, ba, lw["a_log"], lw["dt_bias"], seq_len, F32)
        od, new_state = gdn_states(*stage1, ba, state, lw["a_log"], lw["dt_bias"], lw["dn_norm_w"], seq_len)

    merged = merge_branches(oa, za, od, zb, ga, gb, lw["w_o_attn"], lw["w_o_delta"])
    y = output_block(merged, x, p, lw["w_out"], lw["ple_gate"], lw["ple_proj"], lw["ple_norm_w"])
    return y, ka, va, new_state, new_conv


def kernel(x_prompt, x_sample, cache_k, cache_v, state_delta, state_conv, page_table, p_prompt, p_sample,
           norm_w, w_in, q_norm_w, k_norm_w, conv_w, a_log, dt_bias, dn_norm_w, w_o_attn, w_o_delta, w_out,
           ple_proj, ple_gate, ple_norm_w):
    depth = w_in.shape[0]
    B, L, _ = x_prompt.shape
    DB, DL, _ = x_sample.shape
    yp = x_prompt.reshape(B * L, D_MODEL)
    ys = x_sample.reshape(DB * DL, D_MODEL)
    outs = [[] for _ in range(8)]
    n_pool = cache_k.shape[1]
    ba_start = 4 * WIDTH + CONV_CH + WIDTH
    for i in range(depth):
        wi = w_in[i]
        w_cat = jnp.concatenate([wi[:, :ba_start], wi[:, ba_start + 2 * N_HEADS:],
                                 wi[:, ba_start:ba_start + 2 * N_HEADS],
                                 jnp.zeros((D_MODEL, LANES - 2 * N_HEADS), wi.dtype)], axis=1).astype(BF16)
        pad = lambda v, off: jnp.zeros((1, LANES), F32).at[0, off:off + N_HEADS].set(v.astype(F32))
        lw = dict(norm_w=norm_w[i], w_in=w_cat, q_norm_w=q_norm_w[i], k_norm_w=k_norm_w[i], conv_w=conv_w[i],
                  a_log=pad(a_log[i], N_HEADS), dt_bias=pad(dt_bias[i], N_HEADS), dn_norm_w=dn_norm_w[i],
                  w_o_attn=w_o_attn[i].astype(BF16), w_o_delta=w_o_delta[i].astype(BF16),
                  w_out=w_out[i].astype(BF16), ple_proj=ple_proj[i].astype(BF16),
                  ple_gate=ple_gate[i].astype(BF16), ple_norm_w=ple_norm_w[i],
                  cache_k=cache_k[i].reshape(n_pool, PAGE_ROWS, HEAD_DIM),
                  cache_v=cache_v[i].reshape(n_pool, PAGE_ROWS, HEAD_DIM), page_table=page_table)
        yp, kp, vp, sp, cp = _layer(yp, p_prompt[i].reshape(B * L, PLE_DIM), B, L, jnp.arange(L), lw,
                                    "prompt", None, None)
        halo = jnp.pad(state_conv[i], ((0, 0), (8 - (CONV_WIDTH - 1), 0), (0, 0)))
        ys, ks, vs, ss, cs = _layer(ys, p_sample[i].reshape(DB * DL, PLE_DIM), DB, DL, PAST_LEN + jnp.arange(DL), lw,
                                    "sample", halo, state_delta[i])
        for lst, val in zip(outs, (kp.reshape(B, L, N_HEADS, HEAD_DIM), vp.reshape(B, L, N_HEADS, HEAD_DIM), sp, cp,
                                   ks.reshape(DB, DL, N_HEADS, HEAD_DIM), vs.reshape(DB, DL, N_HEADS, HEAD_DIM), ss, cs)):
            lst.append(val)
    stacked = [jnp.stack(v) for v in outs]
    return (yp.reshape(B, L, D_MODEL), ys.reshape(DB, DL, D_MODEL), *stacked)
```

```python
import functools
import math

import jax
import jax.numpy as jnp
from jax import lax
from jax.experimental import pallas as pl
from jax.experimental.pallas import tpu as pltpu

F32 = jnp.float32
BF16 = jnp.bfloat16
HIGHEST = lax.Precision.HIGHEST

D_MODEL = 2048
HEAD_DIM = 128
N_HEADS = 16
WIDTH = N_HEADS * HEAD_DIM
MOBA_BLOCK = 256
MOBA_TOPK = 3
ROPE_THETA = 10000.0
CONV_CH = 3 * WIDTH
CONV_WIDTH = 4
PLE_DIM = 256
GDN_CHUNK = 64
GDN_PREP_ROWS = 256
NORM_EPS = 1e-6
PAST_LEN = 2048
PAGE_SIZE = 128
PAGE_ROWS = PAGE_SIZE * N_HEADS
LANES = 128
VMEM_LIMIT = 56 * 1024 * 1024

NT_DIMS = (((1,), (1,)), ((), ()))
TN_DIMS = (((0,), (0,)), ((), ()))


def _params(*sem):
    return pltpu.CompilerParams(dimension_semantics=sem, vmem_limit_bytes=VMEM_LIMIT)


def _sigmoid(x):
    return 1.0 / (1.0 + jnp.exp(-x))


def _silu(x):
    return x * _sigmoid(x)


def _softplus(x):
    return jnp.maximum(x, 0.0) + jnp.log(1.0 + jnp.exp(-jnp.abs(x)))


def _rmsnorm_kernel(x_ref, w_ref, o_ref):
    x = x_ref[...]
    ms = jnp.mean(x * x, axis=-1, keepdims=True)
    o_ref[...] = (x * lax.rsqrt(ms + NORM_EPS) * w_ref[...]).astype(o_ref.dtype)


def rmsnorm_bf16(x, w, tm=512):
    T, D = x.shape
    return pl.pallas_call(
        _rmsnorm_kernel,
        out_shape=jax.ShapeDtypeStruct((T, D), BF16),
        grid=(T // tm,),
        in_specs=[pl.BlockSpec((tm, D), lambda i: (i, 0)), pl.BlockSpec((1, D), lambda i: (0, 0))],
        out_specs=pl.BlockSpec((tm, D), lambda i: (i, 0)),
        compiler_params=_params("parallel"),
        name="rmsnorm_bf16",
    )(x, w.reshape(1, D))


def _mm_kernel(h_ref, w_ref, o_ref):
    o_ref[...] = jnp.dot(h_ref[...], w_ref[...], preferred_element_type=F32)


def matmul_cols(h, w, col_start, n_cols, tm, tn):
    T, K = h.shape
    assert col_start % tn == 0 and n_cols % tn == 0 and T % tm == 0
    off = col_start // tn
    return pl.pallas_call(
        _mm_kernel,
        out_shape=jax.ShapeDtypeStruct((T, n_cols), F32),
        grid=(T // tm, n_cols // tn),
        in_specs=[pl.BlockSpec((tm, K), lambda i, j: (i, 0)),
                  pl.BlockSpec((K, tn), lambda i, j: (0, off + j))],
        out_specs=pl.BlockSpec((tm, tn), lambda i, j: (i, j)),
        compiler_params=_params("parallel", "arbitrary"),
        name="in_proj",
    )(h, w)


def _qk_prep_kernel(q_ref, k_ref, v_ref, qw_ref, kw_ref, cos_ref, sin_ref, qo_ref, ko_ref, k3_ref, v3_ref, *km_ref):
    cos = cos_ref[...]
    sin = sin_ref[...]

    def norm_rope(x, w):
        ms = jnp.mean(x * x, axis=-1, keepdims=True)
        y = x * lax.rsqrt(ms + NORM_EPS) * w
        return y * cos + pltpu.roll(y, HEAD_DIM // 2, 1) * sin

    for h in range(N_HEADS):
        sl = slice(h * HEAD_DIM, (h + 1) * HEAD_DIM)
        qo_ref[:, sl] = norm_rope(q_ref[:, sl], qw_ref[...])
        kh = norm_rope(k_ref[:, sl], kw_ref[...])
        ko_ref[:, sl] = kh
        k3_ref[:, h, :] = kh
        v3_ref[:, h, :] = v_ref[:, sl]
        if km_ref:
            km_ref[0][0, :, sl] = jnp.mean(kh, axis=0, keepdims=True)


def qk_prep(qk, v, q_w, k_w, cos, sin, with_kmean):
    T = qk.shape[0]
    tm = MOBA_BLOCK
    n_pos = cos.shape[0] // tm
    flat = pl.BlockSpec((tm, WIDTH), lambda i: (i, 0))
    per_head = pl.BlockSpec((tm, N_HEADS, HEAD_DIM), lambda i: (i, 0, 0))
    out_shape = [jax.ShapeDtypeStruct((T, WIDTH), F32), jax.ShapeDtypeStruct((T, WIDTH), F32),
                 jax.ShapeDtypeStruct((T, N_HEADS, HEAD_DIM), F32), jax.ShapeDtypeStruct((T, N_HEADS, HEAD_DIM), F32)]
    out_specs = [flat, flat, per_head, per_head]
    if with_kmean:
        out_shape.append(jax.ShapeDtypeStruct((T // tm, 1, WIDTH), F32))
        out_specs.append(pl.BlockSpec((1, 1, WIDTH), lambda i: (i, 0, 0)))
    return pl.pallas_call(
        _qk_prep_kernel,
        out_shape=out_shape,
        grid=(T // tm,),
        in_specs=[flat,
                  pl.BlockSpec((tm, WIDTH), lambda i: (i, 1)),
                  flat,
                  pl.BlockSpec((1, HEAD_DIM), lambda i: (0, 0)),
                  pl.BlockSpec((1, HEAD_DIM), lambda i: (0, 0)),
                  pl.BlockSpec((tm, HEAD_DIM), lambda i: (i % n_pos, 0)),
                  pl.BlockSpec((tm, HEAD_DIM), lambda i: (i % n_pos, 0))],
        out_specs=out_specs,
        compiler_params=_params("parallel"),
        name="qk_prep",
    )(qk, qk, v, q_w.reshape(1, HEAD_DIM), k_w.reshape(1, HEAD_DIM), cos, sin)


def rope_tables(pos):
    half = HEAD_DIM // 2
    inv_freq = jnp.power(ROPE_THETA, -jnp.arange(half, dtype=F32) / half)
    ang = pos.astype(F32)[:, None] * inv_freq[None, :]
    cos, sin = jnp.cos(ang), jnp.sin(ang)
    return jnp.concatenate([cos, cos], -1), jnp.concatenate([-sin, sin], -1)


LOG2E = 1.4426950408889634


def _topk_block_mask(gate, n_past):
    blk = lax.broadcasted_iota(jnp.int32, gate.shape, 0)
    cand = blk < n_past
    rows = []
    for n in range(n_past):
        gn = gate[n:n + 1, :]
        beats = (gate > gn) | ((gate == gn) & (blk < n))
        rank = jnp.sum(jnp.where(cand & beats, 1.0, 0.0), axis=0, keepdims=True)
        rows.append(jnp.where(rank < MOBA_TOPK, 1.0, 0.0))
    rows += [jnp.zeros_like(rows[0])] * (gate.shape[0] - n_past)
    return jnp.concatenate(rows, axis=0)


def _moba_prompt_kernel(q_ref, k_ref, v_ref, km_ref, o_ref):
    nb = k_ref.shape[0] // MOBA_BLOCK
    kb = k_ref[...].astype(BF16)
    vb = v_ref[...].astype(BF16)
    km = km_ref[0]
    rpos = lax.broadcasted_iota(jnp.int32, (MOBA_BLOCK, MOBA_BLOCK), 0)
    cpos = lax.broadcasted_iota(jnp.int32, (MOBA_BLOCK, MOBA_BLOCK), 1)
    causal = cpos <= rpos
    def scores(c):
        q = q_ref[c * MOBA_BLOCK:(c + 1) * MOBA_BLOCK, :]
        qb = (q * (HEAD_DIM ** -0.5 * LOG2E)).astype(BF16)
        s = lax.dot_general(qb, kb[:(c + 1) * MOBA_BLOCK, :], NT_DIMS, preferred_element_type=F32)
        gate = (lax.dot_general(km, q, NT_DIMS, precision=HIGHEST, preferred_element_type=F32)
                if c > MOBA_TOPK else None)
        return s, gate

    def probs(c, s, gate):
        if gate is not None:
            keep = _topk_block_mask(gate, c)
            keep = jnp.concatenate([keep, jnp.zeros((LANES - keep.shape[0], MOBA_BLOCK), F32)], axis=0).T
            parts = [jnp.where(keep[:, n:n + 1] > 0.5, s[:, n * MOBA_BLOCK:(n + 1) * MOBA_BLOCK], -jnp.inf)
                     for n in range(c)]
        else:
            parts = [s[:, n * MOBA_BLOCK:(n + 1) * MOBA_BLOCK] for n in range(c)]
        parts.append(jnp.where(causal, s[:, c * MOBA_BLOCK:], -jnp.inf))
        s = jnp.concatenate(parts, axis=1) if c else parts[0]
        p = jnp.exp2(s - jnp.max(s, axis=-1, keepdims=True))
        return p.astype(BF16), jnp.sum(p, axis=-1, keepdims=True)

    order = list(range(nb))
    groups = [[order[i], order[nb - 1 - i]] for i in range(nb // 2)] + ([[order[nb // 2]]] if nb % 2 else [])
    for group in groups:
        sg = [scores(c) for c in group]
        pg = [probs(c, *sg[i]) for i, c in enumerate(group)]
        og = [jnp.dot(pg[i][0], vb[:(c + 1) * MOBA_BLOCK, :], preferred_element_type=F32) for i, c in enumerate(group)]
        for i, c in enumerate(group):
            o_ref[c * MOBA_BLOCK:(c + 1) * MOBA_BLOCK, :] = og[i] * (1.0 / pg[i][1])


def moba_prompt(q, k, v, kmean, batch, seq):
    nb = seq // MOBA_BLOCK
    assert nb <= 8
    strip = pl.BlockSpec((seq, HEAD_DIM), lambda b, h: (b, h))
    return pl.pallas_call(
        _moba_prompt_kernel,
        out_shape=jax.ShapeDtypeStruct((batch * seq, WIDTH), F32),
        grid=(batch, N_HEADS),
        in_specs=[strip, strip, strip, pl.BlockSpec((1, nb, HEAD_DIM), lambda b, h: (b, 0, h))],
        out_specs=strip,
        compiler_params=_params("parallel", "parallel"),
        name="moba_prompt",
    )(q, k, v, kmean)


def _moba_sample_kernel(pt_ref, q_ref, kn_ref, vn_ref, *refs, blocks_per_step):
    page_refs = refs[:4 * blocks_per_step]
    o_ref, g_s, m_s, l_s, o_s = refs[4 * blocks_per_step:]
    step = pl.program_id(1)
    n_q = q_ref.shape[0]
    half = N_HEADS // 2
    pairs = range(half)
    sls = [slice(h * HEAD_DIM, (h + 1) * HEAD_DIM) for h in range(N_HEADS)]
    qs = q_ref[...]
    q_pair = [jnp.concatenate([qs[:, sls[j]], qs[:, sls[j + half]]], axis=0) for j in pairs]
    qb = [(q * (HEAD_DIM ** -0.5 * LOG2E)).astype(BF16) for q in q_pair]
    first = lax.broadcasted_iota(jnp.int32, (2 * n_q, 1), 0) < n_q
    r = lax.broadcasted_iota(jnp.int32, (2 * n_q, MOBA_BLOCK), 0)
    c = lax.broadcasted_iota(jnp.int32, (2 * n_q, MOBA_BLOCK), 1)
    same_head = (r < n_q) == (c % 2 == 0)
    even_row = lax.broadcasted_iota(jnp.int32, (8, HEAD_DIM), 0) % 2 == 0

    def pair_rows(ref, j):
        return ref[0, pl.ds(j, 2 * PAGE_SIZE, stride=half), :]

    def head_sums(k):
        acc = jnp.sum(k.reshape(k.shape[0] // 8, 8, HEAD_DIM), axis=0)
        return (jnp.sum(jnp.where(even_row, acc, 0.0), axis=0, keepdims=True),
                jnp.sum(jnp.where(even_row, 0.0, acc), axis=0, keepdims=True))

    for blk in range(blocks_per_step):
        k0_ref, k1_ref, v0_ref, v1_ref = page_refs[4 * blk:4 * blk + 4]
        n = step * blocks_per_step + blk
        k0 = [pair_rows(k0_ref, j) for j in pairs]
        k1 = [pair_rows(k1_ref, j) for j in pairs]
        s0 = [jnp.where(same_head, lax.dot_general(qb[j], k0[j].astype(BF16), NT_DIMS, preferred_element_type=F32),
                        -jnp.inf) for j in pairs]
        s1 = [jnp.where(same_head, lax.dot_general(qb[j], k1[j].astype(BF16), NT_DIMS, preferred_element_type=F32),
                        -jnp.inf) for j in pairs]
        m = [jnp.max(jnp.maximum(s0[j], s1[j]), axis=-1, keepdims=True) for j in pairs]
        p0 = [jnp.exp2(s0[j] - m[j]) for j in pairs]
        p1 = [jnp.exp2(s1[j] - m[j]) for j in pairs]
        o0 = [jnp.dot(p0[j].astype(BF16), pair_rows(v0_ref, j).astype(BF16), preferred_element_type=F32) for j in pairs]
        o1 = [jnp.dot(p1[j].astype(BF16), pair_rows(v1_ref, j).astype(BF16), preferred_element_type=F32) for j in pairs]
        for j in pairs:
            sum_a, sum_b = head_sums(k0[j] + k1[j])
            kmean = jnp.where(first, sum_a, sum_b) * (1.0 / MOBA_BLOCK)
            gate = jnp.sum(q_pair[j] * kmean, axis=-1, keepdims=True)
            l = jnp.sum(p0[j] + p1[j], axis=-1, keepdims=True)
            g_s[n, j] = jnp.broadcast_to(gate, (2 * n_q, LANES))
            m_s[n, j] = jnp.broadcast_to(m[j], (2 * n_q, LANES))
            l_s[n, j] = jnp.broadcast_to(l, (2 * n_q, LANES))
            o_s[n, j] = o0[j] + o1[j]

    @pl.when(step == pl.num_programs(1) - 1)
    def _():
        nb = g_s.shape[0]
        r = lax.broadcasted_iota(jnp.int32, (2 * n_q, LANES), 0)
        c = lax.broadcasted_iota(jnp.int32, (2 * n_q, LANES), 1)
        own_ok = ((r < n_q) & (c <= r)) | ((r >= n_q) & (c >= n_q) & (c <= r))
        pad = jnp.zeros((LANES - 2 * n_q, HEAD_DIM), F32)
        kn = [jnp.concatenate([kn_ref[:, sls[j]], kn_ref[:, sls[j + half]], pad], axis=0).astype(BF16) for j in pairs]
        vn = [jnp.concatenate([vn_ref[:, sls[j]], vn_ref[:, sls[j + half]], pad], axis=0).astype(BF16) for j in pairs]
        s_own = [jnp.where(own_ok, lax.dot_general(qb[j], kn[j], NT_DIMS, preferred_element_type=F32), -jnp.inf)
                 for j in pairs]
        m_owns = [jnp.max(s, axis=-1, keepdims=True) for s in s_own]
        p_own = [jnp.exp2(s_own[j] - m_owns[j]) for j in pairs]
        o_owns = [jnp.dot(p_own[j].astype(BF16), vn[j], preferred_element_type=F32) for j in pairs]
        for j in pairs:
            m_own = m_owns[j]
            l_own = jnp.sum(p_own[j], axis=-1, keepdims=True)
            keep = []
            for a in range(nb):
                ga = g_s[a, j]
                rank = jnp.zeros((2 * n_q, LANES), F32)
                for b in range(nb):
                    if b != a:
                        gb = g_s[b, j]
                        rank = rank + jnp.where((gb >= ga) if b < a else (gb > ga), 1.0, 0.0)
                keep.append(rank < MOBA_TOPK)
            m_tot = jnp.broadcast_to(m_own, (2 * n_q, LANES))
            for a in range(nb):
                m_tot = jnp.where(keep[a], jnp.maximum(m_tot, m_s[a, j]), m_tot)
            w_own = jnp.exp2(m_own - m_tot)
            l_tot = w_own * l_own
            o_tot = w_own * o_owns[j]
            for a in range(nb):
                w = jnp.where(keep[a], jnp.exp2(m_s[a, j] - m_tot), 0.0)
                l_tot = l_tot + w * l_s[a, j]
                o_tot = o_tot + w * o_s[a, j]
            o_tot = o_tot * (1.0 / l_tot)
            o_ref[:, sls[j]] = o_tot[:n_q]
            o_ref[:, sls[j + half]] = o_tot[n_q:]


def moba_sample(q, k_new, v_new, cache_k, cache_v, page_table, n_q, blocks_per_step=2):
    n_seq, n_pages = page_table.shape
    n_past = n_pages * PAGE_SIZE // MOBA_BLOCK
    assert MOBA_BLOCK == 2 * PAGE_SIZE and n_q == 8 and n_past % blocks_per_step == 0
    seq_spec = pl.BlockSpec((n_q, WIDTH), lambda b, s, pt: (b, 0))

    def page_spec(page_in_step):
        return pl.BlockSpec((1, PAGE_ROWS, HEAD_DIM),
                            lambda b, s, pt: (pt[b * n_pages + 2 * blocks_per_step * s + page_in_step], 0, 0))

    page_specs, page_args = [], []
    for blk in range(blocks_per_step):
        page_specs += [page_spec(2 * blk), page_spec(2 * blk + 1)] * 2
        page_args += [cache_k, cache_k, cache_v, cache_v]
    grid_spec = pltpu.PrefetchScalarGridSpec(
        num_scalar_prefetch=1,
        grid=(n_seq, n_past // blocks_per_step),
        in_specs=[seq_spec, seq_spec, seq_spec] + page_specs,
        out_specs=seq_spec,
        scratch_shapes=[pltpu.VMEM((n_past, N_HEADS // 2, 2 * n_q, LANES), F32) for _ in range(4)],
    )
    return pl.pallas_call(
        functools.partial(_moba_sample_kernel, blocks_per_step=blocks_per_step),
        out_shape=jax.ShapeDtypeStruct((n_seq * n_q, WIDTH), F32),
        grid_spec=grid_spec,
        compiler_params=_params("parallel", "arbitrary"),
        name="moba_sample",
    )(page_table.reshape(-1), q, k_new, v_new, *page_args)


def _gdn_prep_kernel(x_ref, w_ref, *refs, has_halo):
    if has_halo:
        halo_ref, y_ref, c_ref = refs
    else:
        y_ref, c_ref, xp_ref = refs
    c = pl.program_id(1)
    L = x_ref.shape[1]
    w = w_ref[...]
    c_ref[...] = x_ref[:, L - (CONV_WIDTH - 1):, :]

    def conv_silu(cur, prev):
        acc = cur * w[CONV_WIDTH - 1:CONV_WIDTH, :]
        for k in range(1, CONV_WIDTH):
            acc = acc + prev(k) * w[CONV_WIDTH - 1 - k:CONV_WIDTH - k, :]
        return _silu(acc)

    def finish(y, normalise):
        if not normalise:
            return y
        scale = jnp.where(c < N_HEADS, HEAD_DIM ** -0.5, 1.0)
        return y * (lax.rsqrt(jnp.sum(y * y, axis=-1, keepdims=True) + NORM_EPS) * scale)

    def run(normalise):
        if has_halo:
            assert L == 8
            x = x_ref[...]
            halo = halo_ref[...]
            row = lax.broadcasted_iota(jnp.int32, x.shape, 1)
            y = conv_silu(x, lambda k: jnp.where(row < k, pltpu.roll(halo, k, 1), pltpu.roll(x, k, 1)))
            y_ref[...] = finish(y, normalise)
        else:
            xp_ref[0:8, :] = jnp.zeros((8, LANES), F32)
            xp_ref[8:, :] = x_ref[0]
            for r0 in range(0, L, GDN_PREP_ROWS):
                y = conv_silu(xp_ref[8 + r0:8 + r0 + GDN_PREP_ROWS, :],
                              lambda k, r0=r0: xp_ref[8 + r0 - k:8 + r0 - k + GDN_PREP_ROWS, :])
                y_ref[0, r0:r0 + GDN_PREP_ROWS, :] = finish(y, normalise)

    pl.when(c < 2 * N_HEADS)(lambda: run(True))
    pl.when(c >= 2 * N_HEADS)(lambda: run(False))


def gdn_prep(x, conv_w, halo, n_seq_blk):
    n_seq, L, _ = x.shape
    n_strips = CONV_CH // LANES
    in_specs = [pl.BlockSpec((n_seq_blk, L, LANES), lambda s, c: (s, 0, c)),
                pl.BlockSpec((CONV_WIDTH, LANES), lambda s, c: (0, c))]
    args = [x, conv_w]
    if halo is not None:
        in_specs.append(pl.BlockSpec((n_seq_blk, 8, LANES), lambda s, c: (s, 0, c)))
        args.append(halo)
    return pl.pallas_call(
        functools.partial(_gdn_prep_kernel, has_halo=halo is not None),
        out_shape=[jax.ShapeDtypeStruct((n_seq, L, CONV_CH), F32),
                   jax.ShapeDtypeStruct((n_seq, CONV_WIDTH - 1, CONV_CH), F32)],
        grid=(n_seq // n_seq_blk, n_strips),
        in_specs=in_specs,
        out_specs=[pl.BlockSpec((n_seq_blk, L, LANES), lambda s, c: (s, 0, c)),
                   pl.BlockSpec((n_seq_blk, CONV_WIDTH - 1, LANES), lambda s, c: (s, 0, c))],
        scratch_shapes=[] if halo is not None else [pltpu.VMEM((L + 8, LANES), F32)],
        compiler_params=_params("parallel", "parallel"),
        name="gdn_prep",
    )(*args)


def _chunk_cumsum(g, chunk):
    row = lax.broadcasted_iota(jnp.int32, g.shape, 0) % chunk
    shift = 1
    while shift < chunk:
        g = g + jnp.where(row >= shift, pltpu.roll(g, shift, 0), 0.0)
        shift *= 2
    return g


def _gate_terms(ba_ref, alog_ref, dtb_ref, chunk):
    ba = ba_ref[...]
    beta = _sigmoid(ba)
    g = -jnp.exp(alog_ref[...]) * _softplus(ba + dtb_ref[...])
    return beta, _chunk_cumsum(g, chunk)


def _bdot(x, y):
    return jnp.dot(x.astype(BF16), y.astype(BF16), preferred_element_type=F32)


def _gdn_stage1_kernel(q_ref, k_ref, v_ref, ba_ref, alog_ref, dtb_ref,
                       u_ref, w_ref, qe_ref, kdt_ref, qk_ref, gt_ref, *, chunk, hb):
    hg = pl.program_id(1)
    rows = q_ref.shape[0]
    beta, g_cum = _gate_terms(ba_ref, alog_ref, dtb_ref, chunk)
    gt_ref[...] = g_cum.T
    ii = lax.broadcasted_iota(jnp.int32, (rows, rows), 0)
    jj = lax.broadcasted_iota(jnp.int32, (rows, rows), 1)
    same = (ii // chunk) == (jj // chunk)
    incl = same & (ii >= jj)
    strict = same & (ii > jj)
    is_last = jj == (ii // chunk) * chunk + (chunk - 1)
    lane = lax.broadcasted_iota(jnp.int32, (rows, LANES), 1)
    heads = range(hb)
    sls = [slice(hh * HEAD_DIM, (hh + 1) * HEAD_DIM) for hh in heads]
    b_col, g_col, gl_col, decay = [], [], [], []
    for hh in heads:
        head = hg * hb + hh
        b_col.append(jnp.sum(jnp.where(lane == head, beta, 0.0), axis=-1, keepdims=True))
        g_col.append(jnp.sum(jnp.where(lane == head + N_HEADS, g_cum, 0.0), axis=-1, keepdims=True))
        g_row = gt_ref[pl.ds(head + N_HEADS, 1), :]
        gl_col.append(jnp.sum(jnp.where(is_last, g_row, 0.0), axis=-1, keepdims=True))
        decay.append(jnp.where(incl, jnp.exp(jnp.where(incl, g_col[hh] - g_row, 0.0)), 0.0))
    kb = [k_ref[:, sl].astype(BF16) for sl in sls]
    kk = [lax.dot_general(kb[hh], kb[hh], NT_DIMS, preferred_element_type=F32) for hh in heads]
    power = [jnp.where(strict, b_col[hh] * kk[hh] * decay[hh], 0.0) for hh in heads]
    n = [-a for a in power]
    span = 2
    while span < chunk:
        power = [_bdot(p, p) for p in power]
        n = [n[hh] + power[hh] + _bdot(n[hh], power[hh]) for hh in heads]
        span *= 2
    eg = [jnp.exp(g) for g in g_col]
    rv = [v_ref[:, sls[hh]] * b_col[hh] for hh in heads]
    rk = [k_ref[:, sls[hh]] * (b_col[hh] * eg[hh]) for hh in heads]
    nv = [_bdot(n[hh], rv[hh]) for hh in heads]
    nk = [_bdot(n[hh], rk[hh]) for hh in heads]
    qk = [lax.dot_general(q_ref[:, sls[hh]].astype(BF16), kb[hh], NT_DIMS, preferred_element_type=F32) for hh in heads]
    for hh in heads:
        sl = sls[hh]
        u_ref[:, sl] = rv[hh] + nv[hh]
        w_ref[:, sl] = (rk[hh] + nk[hh]).astype(w_ref.dtype)
        qe_ref[:, sl] = (q_ref[:, sl] * eg[hh]).astype(qe_ref.dtype)
        qk_ref[0, sl, :] = (qk[hh] * decay[hh]).astype(qk_ref.dtype)
        kdt_ref[0, sl, :] = (k_ref[:, sl] * jnp.exp(gl_col[hh] - g_col[hh])).T.astype(kdt_ref.dtype)


def gdn_stage1(qkv, ba, a_log_pad, dt_bias_pad, chunk, mid_dtype, hb=8):
    T = qkv.shape[0]
    rows = LANES
    n_hg = N_HEADS // hb
    wblk = hb * HEAD_DIM
    vec = pl.BlockSpec((1, LANES), lambda i, h: (0, 0))
    tok = pl.BlockSpec((rows, wblk), lambda i, h: (i, h))
    blk = pl.BlockSpec((1, wblk, rows), lambda i, h: (i, h, 0))
    return pl.pallas_call(
        functools.partial(_gdn_stage1_kernel, chunk=chunk, hb=hb),
        out_shape=[jax.ShapeDtypeStruct((T, WIDTH), F32),
                   jax.ShapeDtypeStruct((T, WIDTH), mid_dtype),
                   jax.ShapeDtypeStruct((T, WIDTH), mid_dtype),
                   jax.ShapeDtypeStruct((T // rows, WIDTH, rows), mid_dtype),
                   jax.ShapeDtypeStruct((T // rows, WIDTH, rows), mid_dtype)],
        grid=(T // rows, n_hg),
        in_specs=[tok,
                  pl.BlockSpec((rows, wblk), lambda i, h: (i, n_hg + h)),
                  pl.BlockSpec((rows, wblk), lambda i, h: (i, 2 * n_hg + h)),
                  pl.BlockSpec((rows, LANES), lambda i, h: (i, 0)),
                  vec, vec],
        out_specs=[tok, tok, tok, blk, blk],
        scratch_shapes=[pltpu.VMEM((rows, rows), F32)],
        compiler_params=_params("parallel", "parallel"),
        name="gdn_stage1",
    )(qkv, qkv, qkv, ba, a_log_pad, dt_bias_pad)


def _delta_apply(u, w, qe, qk_rows, kdt, v_place, state, decay_last, w_norm):
    n = range(len(state))
    sb = [s.astype(BF16) for s in state]
    ws = [_bdot(w[i], sb[i]) for i in n]
    qs = [_bdot(qe[i], sb[i]) for i in n]
    v_pad = [v_place(u[i] - ws[i]) for i in n]
    o = [qs[i] + _bdot(qk_rows[i], v_pad[i]) for i in n]
    new_state = [state[i] * decay_last[i] + _bdot(kdt[i], v_pad[i]) for i in n]
    o = [x * lax.rsqrt(jnp.mean(x * x, axis=-1, keepdims=True) + NORM_EPS) * w_norm for x in o]
    return o, new_state


def _gdn_carry_kernel(u_ref, w_ref, qe_ref, kdt_ref, qk_ref, ba_ref, alog_ref, dtb_ref, wn_ref,
                      o_ref, s_out_ref, s_ref, *, chunk):
    t = pl.program_id(1)
    rows = u_ref.shape[0]
    n_chunks = rows // chunk

    @pl.when(t == 0)
    def _():
        s_ref[...] = jnp.zeros_like(s_ref)

    _, g_cum = _gate_terms(ba_ref, alog_ref, dtb_ref, chunk)
    w_norm = wn_ref[...]
    zeros = jnp.zeros((chunk, HEAD_DIM), BF16)
    heads = range(N_HEADS)
    sls = [slice(h * HEAD_DIM, (h + 1) * HEAD_DIM) for h in heads]
    state = [s_ref[h] for h in heads]
    for c in range(n_chunks):
        rs = slice(c * chunk, (c + 1) * chunk)
        last = (c + 1) * chunk - 1

        def v_place(v_new, c=c):
            return jnp.concatenate([zeros] * c + [v_new.astype(BF16)] + [zeros] * (n_chunks - 1 - c), axis=0)

        o, state = _delta_apply(
            [u_ref[rs, sl] for sl in sls], [w_ref[rs, sl] for sl in sls], [qe_ref[rs, sl] for sl in sls],
            [qk_ref[0, h * HEAD_DIM + c * chunk:h * HEAD_DIM + (c + 1) * chunk, :] for h in heads],
            [kdt_ref[0, sl, :] for sl in sls], v_place, state,
            [jnp.exp(g_cum[last:last + 1, N_HEADS + h:N_HEADS + h + 1]) for h in heads], w_norm)
        for h in heads:
            o_ref[rs, sls[h]] = o[h]
    for h in heads:
        s_ref[h] = state[h]

    @pl.when(t == pl.num_programs(1) - 1)
    def _():
        s_out_ref[0] = s_ref[...]


def gdn_carry(u, w, qe, kdt, qk, ba, a_log_pad, dt_bias_pad, w_norm, n_seq, chunk):
    T = u.shape[0]
    rows = LANES
    nblk = T // n_seq // rows
    vec = pl.BlockSpec((1, LANES), lambda b, t: (0, 0))
    tok = pl.BlockSpec((rows, WIDTH), lambda b, t: (b * nblk + t, 0))
    blk = pl.BlockSpec((1, WIDTH, rows), lambda b, t: (b * nblk + t, 0, 0))
    return pl.pallas_call(
        functools.partial(_gdn_carry_kernel, chunk=chunk),
        out_shape=[jax.ShapeDtypeStruct((T, WIDTH), F32),
                   jax.ShapeDtypeStruct((n_seq, N_HEADS, HEAD_DIM, HEAD_DIM), F32)],
        grid=(n_seq, nblk),
        in_specs=[tok, tok, tok, blk, blk,
                  pl.BlockSpec((rows, LANES), lambda b, t: (b * nblk + t, 0)), vec, vec, vec],
        out_specs=[tok, pl.BlockSpec((1, N_HEADS, HEAD_DIM, HEAD_DIM), lambda b, t: (b, 0, 0, 0))],
        scratch_shapes=[pltpu.VMEM((N_HEADS, HEAD_DIM, HEAD_DIM), F32)],
        compiler_params=_params("parallel", "arbitrary"),
        name="gdn_carry",
    )(u, w, qe, kdt, qk, ba, a_log_pad, dt_bias_pad, w_norm.reshape(1, HEAD_DIM))


def _gdn_states_kernel(u_ref, w_ref, qe_ref, kdt_ref, qk_ref, ba_ref, s_in_ref, alog_ref, dtb_ref, wn_ref,
                       o_ref, s_out_ref, gcum_s, *, chunk, hb):
    hg = pl.program_id(1)
    rows = u_ref.shape[0]
    n_seq = rows // chunk
    _, g_cum = _gate_terms(ba_ref, alog_ref, dtb_ref, chunk)
    gcum_s[...] = g_cum
    w_norm = wn_ref[...]
    row_seq = lax.broadcasted_iota(jnp.int32, (rows, HEAD_DIM), 0) // chunk
    lane = lax.broadcasted_iota(jnp.int32, (1, LANES), 1)

    heads = range(hb)
    sls = [slice(hh * HEAD_DIM, (hh + 1) * HEAD_DIM) for hh in heads]

    def body(s, carry):
        rs = pl.ds(pl.multiple_of(s * chunk, chunk), chunk)
        g_last_row = gcum_s[pl.ds(s * chunk + (chunk - 1), 1), :]

        def v_place(v_new):
            tiled = jnp.concatenate([v_new] * n_seq, axis=0)
            return jnp.where(row_seq == s, tiled, 0.0).astype(BF16)

        o, state = _delta_apply(
            [u_ref[rs, sl] for sl in sls], [w_ref[rs, sl] for sl in sls], [qe_ref[rs, sl] for sl in sls],
            [qk_ref[0, pl.ds(pl.multiple_of(hh * HEAD_DIM + s * chunk, chunk), chunk), :] for hh in heads],
            [kdt_ref[0, sl, :] for sl in sls], v_place, [s_in_ref[s, hh] for hh in heads],
            [jnp.exp(jnp.sum(jnp.where(lane == hg * hb + hh + N_HEADS, g_last_row, 0.0), axis=-1, keepdims=True))
             for hh in heads], w_norm)
        for hh in heads:
            o_ref[rs, sls[hh]] = o[hh]
            s_out_ref[s, hh] = state[hh]
        return carry

    lax.fori_loop(0, n_seq, body, 0)


def gdn_states(u, w, qe, kdt, qk, ba, state, a_log_pad, dt_bias_pad, w_norm, chunk, hb=8):
    T = u.shape[0]
    rows = LANES
    seq_blk = rows // chunk
    n_hg = N_HEADS // hb
    wblk = hb * HEAD_DIM
    vec = pl.BlockSpec((1, LANES), lambda i, h: (0, 0))
    tok = pl.BlockSpec((rows, wblk), lambda i, h: (i, h))
    blk = pl.BlockSpec((1, wblk, rows), lambda i, h: (i, h, 0))
    st = pl.BlockSpec((seq_blk, hb, HEAD_DIM, HEAD_DIM), lambda i, h: (i, h, 0, 0))
    return pl.pallas_call(
        functools.partial(_gdn_states_kernel, chunk=chunk, hb=hb),
        out_shape=[jax.ShapeDtypeStruct((T, WIDTH), F32), jax.ShapeDtypeStruct(state.shape, F32)],
        grid=(T // rows, n_hg),
        in_specs=[tok, tok, tok, blk, blk, pl.BlockSpec((rows, LANES), lambda i, h: (i, 0)), st, vec, vec, vec],
        out_specs=[tok, st],
        scratch_shapes=[pltpu.VMEM((rows, LANES), F32)],
        compiler_params=_params("parallel", "parallel"),
        name="gdn_states",
    )(u, w, qe, kdt, qk, ba, state, a_log_pad, dt_bias_pad, w_norm.reshape(1, HEAD_DIM))


def _merge_kernel(oa_ref, za_ref, od_ref, zb_ref, ga_ref, gb_ref, wa_ref, wd_ref, o_ref):
    a = (oa_ref[...] * _silu(za_ref[...])).astype(BF16)
    d = (od_ref[...] * _silu(zb_ref[...])).astype(BF16)
    pa = jnp.dot(a, wa_ref[...], preferred_element_type=F32)
    pd = jnp.dot(d, wd_ref[...], preferred_element_type=F32)
    o_ref[...] = (_sigmoid(ga_ref[...]) * pa + _sigmoid(gb_ref[...]) * pd).astype(o_ref.dtype)


def merge_branches(oa, za, od, zb, ga, gb, w_a, w_d, tm=256):
    T = oa.shape[0]
    act = pl.BlockSpec((tm, WIDTH), lambda i: (i, 0))
    wgt = pl.BlockSpec((WIDTH, D_MODEL), lambda i: (0, 0), pipeline_mode=pl.Buffered(1))
    return pl.pallas_call(
        _merge_kernel,
        out_shape=jax.ShapeDtypeStruct((T, D_MODEL), BF16),
        grid=(T // tm,),
        in_specs=[act, act, act, act, act, act, wgt, wgt],
        out_specs=pl.BlockSpec((tm, D_MODEL), lambda i: (i, 0)),
        compiler_params=_params("parallel"),
        name="merge_branches",
    )(oa, za, od, zb, ga, gb, w_a, w_d)


def _out_kernel(m_ref, x_ref, p_ref, wo_ref, wg_ref, wp_ref, nw_ref, y_ref):
    x1 = x_ref[...] + jnp.dot(m_ref[...], wo_ref[...], preferred_element_type=F32)
    ms = jnp.mean(x1 * x1, axis=-1, keepdims=True)
    hn = (x1 * lax.rsqrt(ms + NORM_EPS) * nw_ref[...]).astype(BF16)
    gate = _sigmoid(jnp.dot(hn, wg_ref[...], preferred_element_type=F32))
    emb = jnp.dot(p_ref[...].astype(BF16), wp_ref[...], preferred_element_type=F32)
    y_ref[...] = x1 + gate * emb


def output_block(merged, x, p, w_out, w_gate, w_proj, norm_w, tm=256):
    T = x.shape[0]
    const = lambda shape: pl.BlockSpec(shape, lambda i: (0, 0), pipeline_mode=pl.Buffered(1))
    return pl.pallas_call(
        _out_kernel,
        out_shape=jax.ShapeDtypeStruct((T, D_MODEL), F32),
        grid=(T // tm,),
        in_specs=[pl.BlockSpec((tm, D_MODEL), lambda i: (i, 0)),
                  pl.BlockSpec((tm, D_MODEL), lambda i: (i, 0)),
                  pl.BlockSpec((tm, PLE_DIM), lambda i: (i, 0)),
                  const((D_MODEL, D_MODEL)), const((D_MODEL, D_MODEL)), const((PLE_DIM, D_MODEL)),
                  const((1, D_MODEL))],
        out_specs=pl.BlockSpec((tm, D_MODEL), lambda i: (i, 0)),
        compiler_params=_params("parallel"),
        name="output_block",
    )(merged, x, p, w_out, w_gate, w_proj, norm_w.reshape(1, D_MODEL))


def _layer(x, p, n_seq, seq_len, pos, lw, attend, conv_halo, state):
    T = x.shape[0]
    h = rmsnorm_bf16(x, lw["norm_w"])
    proj = lambda start, n, tn: matmul_cols(h, lw["w_in"], start, n, 1024, tn)
    qk = proj(0, 2 * WIDTH, 1024)
    va = proj(2 * WIDTH, WIDTH, 1024)
    za = proj(3 * WIDTH, WIDTH, 1024)
    conv_in = proj(4 * WIDTH, CONV_CH, 1024)
    zb = proj(4 * WIDTH + CONV_CH, WIDTH, 1024)
    ga = proj(5 * WIDTH + CONV_CH, D_MODEL, 1024)
    gb = proj(5 * WIDTH + CONV_CH + D_MODEL, D_MODEL, 1024)
    ba = proj(5 * WIDTH + CONV_CH + 2 * D_MODEL, LANES, LANES)

    cos, sin = rope_tables(pos)
    if cos.shape[0] < MOBA_BLOCK:
        reps = MOBA_BLOCK // cos.shape[0]
        cos, sin = jnp.tile(cos, (reps, 1)), jnp.tile(sin, (reps, 1))
    qa, ka, k_out, v_out, *kmean = qk_prep(qk, va, lw["q_norm_w"], lw["k_norm_w"], cos, sin,
                                           with_kmean=attend == "prompt")
    if attend == "prompt":
        oa = moba_prompt(qa, ka, va, kmean[0].reshape(n_seq, seq_len // MOBA_BLOCK, WIDTH), n_seq, seq_len)
    else:
        oa = moba_sample(qa, ka, va, lw["cache_k"], lw["cache_v"], lw["page_table"], seq_len)

    qkv, new_conv = gdn_prep(conv_in.reshape(n_seq, seq_len, CONV_CH), lw["conv_w"], conv_halo,
                             n_seq_blk=1 if conv_halo is None else min(n_seq, 128))
    qkv = qkv.reshape(T, CONV_CH)
    if state is None:
        stage1 = gdn_stage1(qkv, ba, lw["a_log"], lw["dt_bias"], GDN_CHUNK, BF16)
        od, new_state = gdn_carry(*stage1, ba, lw["a_log"], lw["dt_bias"], lw["dn_norm_w"], n_seq, GDN_CHUNK)
    else:
        stage1 = gdn_stage1(qkv, ba, lw["a_log"], lw["dt_bias"], seq_len, F32)
        od, new_state = gdn_states(*stage1, ba, state, lw["a_log"], lw["dt_bias"], lw["dn_norm_w"], seq_len)

    merged = merge_branches(oa, za, od, zb, ga, gb, lw["w_o_attn"], lw["w_o_delta"])
    y = output_block(merged, x, p, lw["w_out"], lw["ple_gate"], lw["ple_proj"], lw["ple_norm_w"])
    return y, k_out, v_out, new_state, new_conv


def kernel(x_prompt, x_sample, cache_k, cache_v, state_delta, state_conv, page_table, p_prompt, p_sample,
           norm_w, w_in, q_norm_w, k_norm_w, conv_w, a_log, dt_bias, dn_norm_w, w_o_attn, w_o_delta, w_out,
           ple_proj, ple_gate, ple_norm_w):
    depth = w_in.shape[0]
    B, L, _ = x_prompt.shape
    DB, DL, _ = x_sample.shape
    yp = x_prompt.reshape(B * L, D_MODEL)
    ys = x_sample.reshape(DB * DL, D_MODEL)
    outs = [[] for _ in range(8)]
    n_pool = cache_k.shape[1]
    ba_start = 4 * WIDTH + CONV_CH + WIDTH
    for i in range(depth):
        wi = w_in[i]
        w_cat = jnp.concatenate([wi[:, :ba_start], wi[:, ba_start + 2 * N_HEADS:],
                                 wi[:, ba_start:ba_start + 2 * N_HEADS],
                                 jnp.zeros((D_MODEL, LANES - 2 * N_HEADS), wi.dtype)], axis=1).astype(BF16)
        pad = lambda v, off: jnp.zeros((1, LANES), F32).at[0, off:off + N_HEADS].set(v.astype(F32))
        lw = dict(norm_w=norm_w[i], w_in=w_cat, q_norm_w=q_norm_w[i], k_norm_w=k_norm_w[i], conv_w=conv_w[i],
                  a_log=pad(a_log[i], N_HEADS), dt_bias=pad(dt_bias[i], N_HEADS), dn_norm_w=dn_norm_w[i],
                  w_o_attn=w_o_attn[i].astype(BF16), w_o_delta=w_o_delta[i].astype(BF16),
                  w_out=w_out[i].astype(BF16), ple_proj=ple_proj[i].astype(BF16),
                  ple_gate=ple_gate[i].astype(BF16), ple_norm_w=ple_norm_w[i],
                  cache_k=cache_k[i].reshape(n_pool, PAGE_ROWS, HEAD_DIM),
                  cache_v=cache_v[i].reshape(n_pool, PAGE_ROWS, HEAD_DIM), page_table=page_table)
        yp, kp, vp, sp, cp = _layer(yp, p_prompt[i].reshape(B * L, PLE_DIM), B, L, jnp.arange(L), lw,
                                    "prompt", None, None)
        halo = jnp.pad(state_conv[i], ((0, 0), (8 - (CONV_WIDTH - 1), 0), (0, 0)))
        ys, ks, vs, ss, cs = _layer(ys, p_sample[i].reshape(DB * DL, PLE_DIM), DB, DL, PAST_LEN + jnp.arange(DL), lw,
                                    "sample", halo, state_delta[i])
        for lst, val in zip(outs, (kp.reshape(B, L, N_HEADS, HEAD_DIM), vp.reshape(B, L, N_HEADS, HEAD_DIM), sp, cp,
                                   ks.reshape(DB, DL, N_HEADS, HEAD_DIM), vs.reshape(DB, DL, N_HEADS, HEAD_DIM), ss, cs)):
            lst.append(val)
    stacked = [jnp.stack(v) for v in outs]
    return (yp.reshape(B, L, D_MODEL), ys.reshape(DB, DL, D_MODEL), *stacked)
```

```python
import functools
import math

import jax
import jax.numpy as jnp
from jax import lax
from jax.experimental import pallas as pl
from jax.experimental.pallas import tpu as pltpu

F32 = jnp.float32
BF16 = jnp.bfloat16
HIGHEST = lax.Precision.HIGHEST

D_MODEL = 2048
HEAD_DIM = 128
N_HEADS = 16
WIDTH = N_HEADS * HEAD_DIM
MOBA_BLOCK = 256
MOBA_TOPK = 3
ROPE_THETA = 10000.0
CONV_CH = 3 * WIDTH
CONV_WIDTH = 4
PLE_DIM = 256
GDN_CHUNK = 64
GDN_PREP_ROWS = 256
IN_PROJ_ROWS = 2048
NORM_EPS = 1e-6
PAST_LEN = 2048
PAGE_SIZE = 128
PAGE_ROWS = PAGE_SIZE * N_HEADS
LANES = 128
VMEM_LIMIT = 56 * 1024 * 1024

NT_DIMS = (((1,), (1,)), ((), ()))
TN_DIMS = (((0,), (0,)), ((), ()))


def _params(*sem):
    return pltpu.CompilerParams(dimension_semantics=sem, vmem_limit_bytes=VMEM_LIMIT)


def _sigmoid(x):
    return 1.0 / (1.0 + jnp.exp(-x))


def _silu(x):
    return x * _sigmoid(x)


def _softplus(x):
    return jnp.maximum(x, 0.0) + jnp.log(1.0 + jnp.exp(-jnp.abs(x)))


def _rmsnorm_kernel(x_ref, w_ref, o_ref):
    x = x_ref[...]
    ms = jnp.mean(x * x, axis=-1, keepdims=True)
    o_ref[...] = (x * lax.rsqrt(ms + NORM_EPS) * w_ref[...]).astype(o_ref.dtype)


def rmsnorm_bf16(x, w, tm=512):
    T, D = x.shape
    return pl.pallas_call(
        _rmsnorm_kernel,
        out_shape=jax.ShapeDtypeStruct((T, D), BF16),
        grid=(T // tm,),
        in_specs=[pl.BlockSpec((tm, D), lambda i: (i, 0)), pl.BlockSpec((1, D), lambda i: (0, 0))],
        out_specs=pl.BlockSpec((tm, D), lambda i: (i, 0)),
        compiler_params=_params("parallel"),
        name="rmsnorm_bf16",
    )(x, w.reshape(1, D))


def _mm_kernel(h_ref, w_ref, o_ref):
    o_ref[...] = jnp.dot(h_ref[...], w_ref[...], preferred_element_type=F32).astype(o_ref.dtype)


def matmul_cols(h, w, col_start, n_cols, tm, tn, out_dtype=F32):
    T, K = h.shape
    tm = min(tm, T)
    assert col_start % tn == 0 and n_cols % tn == 0 and T % tm == 0
    off = col_start // tn
    return pl.pallas_call(
        _mm_kernel,
        out_shape=jax.ShapeDtypeStruct((T, n_cols), out_dtype),
        grid=(T // tm, n_cols // tn),
        in_specs=[pl.BlockSpec((tm, K), lambda i, j: (i, 0)),
                  pl.BlockSpec((K, tn), lambda i, j: (0, off + j))],
        out_specs=pl.BlockSpec((tm, tn), lambda i, j: (i, j)),
        compiler_params=_params("parallel", "arbitrary"),
        name="in_proj",
    )(h, w)


def _qk_prep_kernel(q_ref, k_ref, v_ref, qw_ref, kw_ref, cos_ref, sin_ref, qo_ref, ko_ref, k3_ref, v3_ref, *km_ref):
    cos = cos_ref[...]
    sin = sin_ref[...]

    def norm_rope(x, w):
        ms = jnp.mean(x * x, axis=-1, keepdims=True)
        y = x * lax.rsqrt(ms + NORM_EPS) * w
        return y * cos + pltpu.roll(y, HEAD_DIM // 2, 1) * sin

    for h in range(N_HEADS):
        sl = slice(h * HEAD_DIM, (h + 1) * HEAD_DIM)
        qo_ref[:, sl] = norm_rope(q_ref[:, sl], qw_ref[...])
        kh = norm_rope(k_ref[:, sl], kw_ref[...])
        ko_ref[:, sl] = kh
        k3_ref[:, h, :] = kh
        v3_ref[:, h, :] = v_ref[:, sl]
        if km_ref:
            km_ref[0][0, :, sl] = jnp.mean(kh, axis=0, keepdims=True)


def qk_prep(qk, v, q_w, k_w, cos, sin, with_kmean):
    T = qk.shape[0]
    tm = MOBA_BLOCK
    n_pos = cos.shape[0] // tm
    flat = pl.BlockSpec((tm, WIDTH), lambda i: (i, 0))
    per_head = pl.BlockSpec((tm, N_HEADS, HEAD_DIM), lambda i: (i, 0, 0))
    out_shape = [jax.ShapeDtypeStruct((T, WIDTH), F32), jax.ShapeDtypeStruct((T, WIDTH), F32),
                 jax.ShapeDtypeStruct((T, N_HEADS, HEAD_DIM), F32), jax.ShapeDtypeStruct((T, N_HEADS, HEAD_DIM), F32)]
    out_specs = [flat, flat, per_head, per_head]
    if with_kmean:
        out_shape.append(jax.ShapeDtypeStruct((T // tm, 1, WIDTH), F32))
        out_specs.append(pl.BlockSpec((1, 1, WIDTH), lambda i: (i, 0, 0)))
    return pl.pallas_call(
        _qk_prep_kernel,
        out_shape=out_shape,
        grid=(T // tm,),
        in_specs=[flat,
                  pl.BlockSpec((tm, WIDTH), lambda i: (i, 1)),
                  flat,
                  pl.BlockSpec((1, HEAD_DIM), lambda i: (0, 0)),
                  pl.BlockSpec((1, HEAD_DIM), lambda i: (0, 0)),
                  pl.BlockSpec((tm, HEAD_DIM), lambda i: (i % n_pos, 0)),
                  pl.BlockSpec((tm, HEAD_DIM), lambda i: (i % n_pos, 0))],
        out_specs=out_specs,
        compiler_params=_params("parallel"),
        name="qk_prep",
    )(qk, qk, v, q_w.reshape(1, HEAD_DIM), k_w.reshape(1, HEAD_DIM), cos, sin)


def rope_tables(pos):
    half = HEAD_DIM // 2
    inv_freq = jnp.power(ROPE_THETA, -jnp.arange(half, dtype=F32) / half)
    ang = pos.astype(F32)[:, None] * inv_freq[None, :]
    cos, sin = jnp.cos(ang), jnp.sin(ang)
    return jnp.concatenate([cos, cos], -1), jnp.concatenate([-sin, sin], -1)


LOG2E = 1.4426950408889634


def _topk_block_mask(gate, n_past):
    blk = lax.broadcasted_iota(jnp.int32, gate.shape, 0)
    cand = blk < n_past
    rows = []
    for n in range(n_past):
        gn = gate[n:n + 1, :]
        beats = (gate > gn) | ((gate == gn) & (blk < n))
        rank = jnp.sum(jnp.where(cand & beats, 1.0, 0.0), axis=0, keepdims=True)
        rows.append(jnp.where(rank < MOBA_TOPK, 1.0, 0.0))
    rows += [jnp.zeros_like(rows[0])] * (gate.shape[0] - n_past)
    return jnp.concatenate(rows, axis=0)


def _moba_prompt_kernel(q_ref, k_ref, v_ref, km_ref, o_ref):
    nb = k_ref.shape[0] // MOBA_BLOCK
    kb = k_ref[...].astype(BF16)
    vb = v_ref[...].astype(BF16)
    km = km_ref[0]
    rpos = lax.broadcasted_iota(jnp.int32, (MOBA_BLOCK, MOBA_BLOCK), 0)
    cpos = lax.broadcasted_iota(jnp.int32, (MOBA_BLOCK, MOBA_BLOCK), 1)
    causal = cpos <= rpos
    def scores(c):
        q = q_ref[c * MOBA_BLOCK:(c + 1) * MOBA_BLOCK, :]
        qb = (q * (HEAD_DIM ** -0.5 * LOG2E)).astype(BF16)
        s = lax.dot_general(qb, kb[:(c + 1) * MOBA_BLOCK, :], NT_DIMS, preferred_element_type=F32)
        gate = (lax.dot_general(km, q, NT_DIMS, precision=HIGHEST, preferred_element_type=F32)
                if c > MOBA_TOPK else None)
        return s, gate

    def probs(c, s, gate):
        if gate is not None:
            keep = _topk_block_mask(gate, c)
            keep = jnp.concatenate([keep, jnp.zeros((LANES - keep.shape[0], MOBA_BLOCK), F32)], axis=0).T
            parts = [jnp.where(keep[:, n:n + 1] > 0.5, s[:, n * MOBA_BLOCK:(n + 1) * MOBA_BLOCK], -jnp.inf)
                     for n in range(c)]
        else:
            parts = [s[:, n * MOBA_BLOCK:(n + 1) * MOBA_BLOCK] for n in range(c)]
        parts.append(jnp.where(causal, s[:, c * MOBA_BLOCK:], -jnp.inf))
        s = jnp.concatenate(parts, axis=1) if c else parts[0]
        p = jnp.exp2(s - jnp.max(s, axis=-1, keepdims=True))
        return p.astype(BF16), jnp.sum(p, axis=-1, keepdims=True)

    order = list(range(nb))
    groups = [[order[i], order[nb - 1 - i]] for i in range(nb // 2)] + ([[order[nb // 2]]] if nb % 2 else [])
    for group in groups:
        sg = [scores(c) for c in group]
        pg = [probs(c, *sg[i]) for i, c in enumerate(group)]
        og = [jnp.dot(pg[i][0], vb[:(c + 1) * MOBA_BLOCK, :], preferred_element_type=F32) for i, c in enumerate(group)]
        for i, c in enumerate(group):
            o_ref[c * MOBA_BLOCK:(c + 1) * MOBA_BLOCK, :] = (og[i] * (1.0 / pg[i][1])).astype(o_ref.dtype)


def moba_prompt(q, k, v, kmean, batch, seq):
    nb = seq // MOBA_BLOCK
    assert nb <= 8
    strip = pl.BlockSpec((seq, HEAD_DIM), lambda b, h: (b, h))
    return pl.pallas_call(
        _moba_prompt_kernel,
        out_shape=jax.ShapeDtypeStruct((batch * seq, WIDTH), BF16),
        grid=(batch, N_HEADS),
        in_specs=[strip, strip, strip, pl.BlockSpec((1, nb, HEAD_DIM), lambda b, h: (b, 0, h))],
        out_specs=strip,
        compiler_params=_params("parallel", "parallel"),
        name="moba_prompt",
    )(q, k, v, kmean)


def _moba_sample_kernel(pt_ref, q_ref, kn_ref, vn_ref, *refs, blocks_per_step):
    page_refs = refs[:4 * blocks_per_step]
    o_ref, g_s, m_s, l_s, o_s = refs[4 * blocks_per_step:]
    step = pl.program_id(1)
    n_q = q_ref.shape[0]
    half = N_HEADS // 2
    pairs = range(half)
    sls = [slice(h * HEAD_DIM, (h + 1) * HEAD_DIM) for h in range(N_HEADS)]
    qs = q_ref[...]
    q_pair = [jnp.concatenate([qs[:, sls[j]], qs[:, sls[j + half]]], axis=0) for j in pairs]
    qb = [(q * (HEAD_DIM ** -0.5 * LOG2E)).astype(BF16) for q in q_pair]
    first = lax.broadcasted_iota(jnp.int32, (2 * n_q, 1), 0) < n_q
    r = lax.broadcasted_iota(jnp.int32, (2 * n_q, MOBA_BLOCK), 0)
    c = lax.broadcasted_iota(jnp.int32, (2 * n_q, MOBA_BLOCK), 1)
    same_head = (r < n_q) == (c % 2 == 0)
    even_row = lax.broadcasted_iota(jnp.int32, (8, HEAD_DIM), 0) % 2 == 0

    def pair_rows(ref, j):
        return ref[0, pl.ds(j, 2 * PAGE_SIZE, stride=half), :]

    def head_sums(k):
        acc = jnp.sum(k.reshape(k.shape[0] // 8, 8, HEAD_DIM), axis=0)
        return (jnp.sum(jnp.where(even_row, acc, 0.0), axis=0, keepdims=True),
                jnp.sum(jnp.where(even_row, 0.0, acc), axis=0, keepdims=True))

    for blk in range(blocks_per_step):
        k0_ref, k1_ref, v0_ref, v1_ref = page_refs[4 * blk:4 * blk + 4]
        n = step * blocks_per_step + blk
        k0 = [pair_rows(k0_ref, j) for j in pairs]
        k1 = [pair_rows(k1_ref, j) for j in pairs]
        s0 = [jnp.where(same_head, lax.dot_general(qb[j], k0[j].astype(BF16), NT_DIMS, preferred_element_type=F32),
                        -jnp.inf) for j in pairs]
        s1 = [jnp.where(same_head, lax.dot_general(qb[j], k1[j].astype(BF16), NT_DIMS, preferred_element_type=F32),
                        -jnp.inf) for j in pairs]
        m = [jnp.max(jnp.maximum(s0[j], s1[j]), axis=-1, keepdims=True) for j in pairs]
        p0 = [jnp.exp2(s0[j] - m[j]) for j in pairs]
        p1 = [jnp.exp2(s1[j] - m[j]) for j in pairs]
        o0 = [jnp.dot(p0[j].astype(BF16), pair_rows(v0_ref, j).astype(BF16), preferred_element_type=F32) for j in pairs]
        o1 = [jnp.dot(p1[j].astype(BF16), pair_rows(v1_ref, j).astype(BF16), preferred_element_type=F32) for j in pairs]
        for j in pairs:
            sum_a, sum_b = head_sums(k0[j] + k1[j])
            kmean = jnp.where(first, sum_a, sum_b) * (1.0 / MOBA_BLOCK)
            gate = jnp.sum(q_pair[j] * kmean, axis=-1, keepdims=True)
            l = jnp.sum(p0[j] + p1[j], axis=-1, keepdims=True)
            g_s[n, j] = jnp.broadcast_to(gate, (2 * n_q, LANES))
            m_s[n, j] = jnp.broadcast_to(m[j], (2 * n_q, LANES))
            l_s[n, j] = jnp.broadcast_to(l, (2 * n_q, LANES))
            o_s[n, j] = o0[j] + o1[j]

    @pl.when(step == pl.num_programs(1) - 1)
    def _():
        nb = g_s.shape[0]
        r = lax.broadcasted_iota(jnp.int32, (2 * n_q, LANES), 0)
        c = lax.broadcasted_iota(jnp.int32, (2 * n_q, LANES), 1)
        own_ok = ((r < n_q) & (c <= r)) | ((r >= n_q) & (c >= n_q) & (c <= r))
        pad = jnp.zeros((LANES - 2 * n_q, HEAD_DIM), F32)
        kn = [jnp.concatenate([kn_ref[:, sls[j]], kn_ref[:, sls[j + half]], pad], axis=0).astype(BF16) for j in pairs]
        vn = [jnp.concatenate([vn_ref[:, sls[j]], vn_ref[:, sls[j + half]], pad], axis=0).astype(BF16) for j in pairs]
        s_own = [jnp.where(own_ok, lax.dot_general(qb[j], kn[j], NT_DIMS, preferred_element_type=F32), -jnp.inf)
                 for j in pairs]
        m_owns = [jnp.max(s, axis=-1, keepdims=True) for s in s_own]
        p_own = [jnp.exp2(s_own[j] - m_owns[j]) for j in pairs]
        o_owns = [jnp.dot(p_own[j].astype(BF16), vn[j], preferred_element_type=F32) for j in pairs]
        for j in pairs:
            m_own = m_owns[j]
            l_own = jnp.sum(p_own[j], axis=-1, keepdims=True)
            keep = []
            for a in range(nb):
                ga = g_s[a, j]
                rank = jnp.zeros((2 * n_q, LANES), F32)
                for b in range(nb):
                    if b != a:
                        gb = g_s[b, j]
                        rank = rank + jnp.where((gb >= ga) if b < a else (gb > ga), 1.0, 0.0)
                keep.append(rank < MOBA_TOPK)
            m_tot = jnp.broadcast_to(m_own, (2 * n_q, LANES))
            for a in range(nb):
                m_tot = jnp.where(keep[a], jnp.maximum(m_tot, m_s[a, j]), m_tot)
            w_own = jnp.exp2(m_own - m_tot)
            l_tot = w_own * l_own
            o_tot = w_own * o_owns[j]
            for a in range(nb):
                w = jnp.where(keep[a], jnp.exp2(m_s[a, j] - m_tot), 0.0)
                l_tot = l_tot + w * l_s[a, j]
                o_tot = o_tot + w * o_s[a, j]
            o_tot = o_tot * (1.0 / l_tot)
            o_ref[:, sls[j]] = o_tot[:n_q]
            o_ref[:, sls[j + half]] = o_tot[n_q:]


def moba_sample(q, k_new, v_new, cache_k, cache_v, page_table, n_q, blocks_per_step=2):
    n_seq, n_pages = page_table.shape
    n_past = n_pages * PAGE_SIZE // MOBA_BLOCK
    assert MOBA_BLOCK == 2 * PAGE_SIZE and n_q == 8 and n_past % blocks_per_step == 0
    seq_spec = pl.BlockSpec((n_q, WIDTH), lambda b, s, pt: (b, 0))

    def page_spec(page_in_step):
        return pl.BlockSpec((1, PAGE_ROWS, HEAD_DIM),
                            lambda b, s, pt: (pt[b * n_pages + 2 * blocks_per_step * s + page_in_step], 0, 0))

    page_specs, page_args = [], []
    for blk in range(blocks_per_step):
        page_specs += [page_spec(2 * blk), page_spec(2 * blk + 1)] * 2
        page_args += [cache_k, cache_k, cache_v, cache_v]
    grid_spec = pltpu.PrefetchScalarGridSpec(
        num_scalar_prefetch=1,
        grid=(n_seq, n_past // blocks_per_step),
        in_specs=[seq_spec, seq_spec, seq_spec] + page_specs,
        out_specs=seq_spec,
        scratch_shapes=[pltpu.VMEM((n_past, N_HEADS // 2, 2 * n_q, LANES), F32) for _ in range(4)],
    )
    return pl.pallas_call(
        functools.partial(_moba_sample_kernel, blocks_per_step=blocks_per_step),
        out_shape=jax.ShapeDtypeStruct((n_seq * n_q, WIDTH), F32),
        grid_spec=grid_spec,
        compiler_params=_params("parallel", "arbitrary"),
        name="moba_sample",
    )(page_table.reshape(-1), q, k_new, v_new, *page_args)


def _gdn_prep_kernel(x_ref, w_ref, *refs, has_halo):
    if has_halo:
        halo_ref, y_ref, c_ref = refs
    else:
        y_ref, c_ref, xp_ref = refs
    c = pl.program_id(1)
    L = x_ref.shape[1]
    w = w_ref[...]
    c_ref[...] = x_ref[:, L - (CONV_WIDTH - 1):, :]

    def conv_silu(cur, prev):
        acc = cur * w[CONV_WIDTH - 1:CONV_WIDTH, :]
        for k in range(1, CONV_WIDTH):
            acc = acc + prev(k) * w[CONV_WIDTH - 1 - k:CONV_WIDTH - k, :]
        return _silu(acc)

    def finish(y, normalise):
        if not normalise:
            return y
        scale = jnp.where(c < N_HEADS, HEAD_DIM ** -0.5, 1.0)
        return y * (lax.rsqrt(jnp.sum(y * y, axis=-1, keepdims=True) + NORM_EPS) * scale)

    def run(normalise):
        if has_halo:
            assert L == 8
            x = x_ref[...]
            halo = halo_ref[...]
            row = lax.broadcasted_iota(jnp.int32, x.shape, 1)
            y = conv_silu(x, lambda k: jnp.where(row < k, pltpu.roll(halo, k, 1), pltpu.roll(x, k, 1)))
            y_ref[...] = finish(y, normalise)
        else:
            xp_ref[0:8, :] = jnp.zeros((8, LANES), F32)
            xp_ref[8:, :] = x_ref[0]
            for r0 in range(0, L, GDN_PREP_ROWS):
                y = conv_silu(xp_ref[8 + r0:8 + r0 + GDN_PREP_ROWS, :],
                              lambda k, r0=r0: xp_ref[8 + r0 - k:8 + r0 - k + GDN_PREP_ROWS, :])
                y_ref[0, r0:r0 + GDN_PREP_ROWS, :] = finish(y, normalise)

    pl.when(c < 2 * N_HEADS)(lambda: run(True))
    pl.when(c >= 2 * N_HEADS)(lambda: run(False))


def gdn_prep(x, conv_w, halo, n_seq_blk):
    n_seq, L, _ = x.shape
    n_strips = CONV_CH // LANES
    in_specs = [pl.BlockSpec((n_seq_blk, L, LANES), lambda s, c: (s, 0, c)),
                pl.BlockSpec((CONV_WIDTH, LANES), lambda s, c: (0, c))]
    args = [x, conv_w]
    if halo is not None:
        in_specs.append(pl.BlockSpec((n_seq_blk, 8, LANES), lambda s, c: (s, 0, c)))
        args.append(halo)
    return pl.pallas_call(
        functools.partial(_gdn_prep_kernel, has_halo=halo is not None),
        out_shape=[jax.ShapeDtypeStruct((n_seq, L, CONV_CH), F32),
                   jax.ShapeDtypeStruct((n_seq, CONV_WIDTH - 1, CONV_CH), F32)],
        grid=(n_seq // n_seq_blk, n_strips),
        in_specs=in_specs,
        out_specs=[pl.BlockSpec((n_seq_blk, L, LANES), lambda s, c: (s, 0, c)),
                   pl.BlockSpec((n_seq_blk, CONV_WIDTH - 1, LANES), lambda s, c: (s, 0, c))],
        scratch_shapes=[] if halo is not None else [pltpu.VMEM((L + 8, LANES), F32)],
        compiler_params=_params("parallel", "parallel"),
        name="gdn_prep",
    )(*args)


def _chunk_cumsum(g, chunk):
    row = lax.broadcasted_iota(jnp.int32, g.shape, 0) % chunk
    shift = 1
    while shift < chunk:
        g = g + jnp.where(row >= shift, pltpu.roll(g, shift, 0), 0.0)
        shift *= 2
    return g


def _gate_terms(ba_ref, alog_ref, dtb_ref, chunk):
    ba = ba_ref[...]
    beta = _sigmoid(ba)
    g = -jnp.exp(alog_ref[...]) * _softplus(ba + dtb_ref[...])
    return beta, _chunk_cumsum(g, chunk)


def _bdot(x, y):
    return jnp.dot(x.astype(BF16), y.astype(BF16), preferred_element_type=F32)


def _gdn_stage1_kernel(q_ref, k_ref, v_ref, ba_ref, alog_ref, dtb_ref,
                       u_ref, w_ref, qe_ref, kdt_ref, qk_ref, gt_ref, *, chunk, hb):
    hg = pl.program_id(1)
    rows = q_ref.shape[0]
    beta, g_cum = _gate_terms(ba_ref, alog_ref, dtb_ref, chunk)
    gt_ref[...] = g_cum.T
    ii = lax.broadcasted_iota(jnp.int32, (rows, rows), 0)
    jj = lax.broadcasted_iota(jnp.int32, (rows, rows), 1)
    same = (ii // chunk) == (jj // chunk)
    incl = same & (ii >= jj)
    strict = same & (ii > jj)
    is_last = jj == (ii // chunk) * chunk + (chunk - 1)
    lane = lax.broadcasted_iota(jnp.int32, (rows, LANES), 1)
    heads = range(hb)
    sls = [slice(hh * HEAD_DIM, (hh + 1) * HEAD_DIM) for hh in heads]
    b_col, g_col, gl_col, decay = [], [], [], []
    for hh in heads:
        head = hg * hb + hh
        b_col.append(jnp.sum(jnp.where(lane == head, beta, 0.0), axis=-1, keepdims=True))
        g_col.append(jnp.sum(jnp.where(lane == head + N_HEADS, g_cum, 0.0), axis=-1, keepdims=True))
        g_row = gt_ref[pl.ds(head + N_HEADS, 1), :]
        gl_col.append(jnp.sum(jnp.where(is_last, g_row, 0.0), axis=-1, keepdims=True))
        decay.append(jnp.where(incl, jnp.exp(jnp.where(incl, g_col[hh] - g_row, 0.0)), 0.0))
    kb = [k_ref[:, sl].astype(BF16) for sl in sls]
    kk = [lax.dot_general(kb[hh], kb[hh], NT_DIMS, preferred_element_type=F32) for hh in heads]
    power = [jnp.where(strict, b_col[hh] * kk[hh] * decay[hh], 0.0) for hh in heads]
    n = [-a for a in power]
    span = 2
    while span < chunk:
        power = [_bdot(p, p) for p in power]
        n = [n[hh] + power[hh] + _bdot(n[hh], power[hh]) for hh in heads]
        span *= 2
    eg = [jnp.exp(g) for g in g_col]
    rv = [v_ref[:, sls[hh]] * b_col[hh] for hh in heads]
    rk = [k_ref[:, sls[hh]] * (b_col[hh] * eg[hh]) for hh in heads]
    nv = [_bdot(n[hh], rv[hh]) for hh in heads]
    nk = [_bdot(n[hh], rk[hh]) for hh in heads]
    qk = [lax.dot_general(q_ref[:, sls[hh]].astype(BF16), kb[hh], NT_DIMS, preferred_element_type=F32) for hh in heads]
    for hh in heads:
        sl = sls[hh]
        u_ref[:, sl] = rv[hh] + nv[hh]
        w_ref[:, sl] = (rk[hh] + nk[hh]).astype(w_ref.dtype)
        qe_ref[:, sl] = (q_ref[:, sl] * eg[hh]).astype(qe_ref.dtype)
        qk_ref[0, sl, :] = (qk[hh] * decay[hh]).astype(qk_ref.dtype)
        kdt_ref[0, sl, :] = (k_ref[:, sl] * jnp.exp(gl_col[hh] - g_col[hh])).T.astype(kdt_ref.dtype)


def gdn_stage1(qkv, ba, a_log_pad, dt_bias_pad, chunk, mid_dtype, hb=8):
    T = qkv.shape[0]
    rows = LANES
    n_hg = N_HEADS // hb
    wblk = hb * HEAD_DIM
    vec = pl.BlockSpec((1, LANES), lambda i, h: (0, 0))
    tok = pl.BlockSpec((rows, wblk), lambda i, h: (i, h))
    blk = pl.BlockSpec((1, wblk, rows), lambda i, h: (i, h, 0))
    return pl.pallas_call(
        functools.partial(_gdn_stage1_kernel, chunk=chunk, hb=hb),
        out_shape=[jax.ShapeDtypeStruct((T, WIDTH), F32),
                   jax.ShapeDtypeStruct((T, WIDTH), mid_dtype),
                   jax.ShapeDtypeStruct((T, WIDTH), mid_dtype),
                   jax.ShapeDtypeStruct((T // rows, WIDTH, rows), mid_dtype),
                   jax.ShapeDtypeStruct((T // rows, WIDTH, rows), mid_dtype)],
        grid=(T // rows, n_hg),
        in_specs=[tok,
                  pl.BlockSpec((rows, wblk), lambda i, h: (i, n_hg + h)),
                  pl.BlockSpec((rows, wblk), lambda i, h: (i, 2 * n_hg + h)),
                  pl.BlockSpec((rows, LANES), lambda i, h: (i, 0)),
                  vec, vec],
        out_specs=[tok, tok, tok, blk, blk],
        scratch_shapes=[pltpu.VMEM((rows, rows), F32)],
        compiler_params=_params("parallel", "parallel"),
        name="gdn_stage1",
    )(qkv, qkv, qkv, ba, a_log_pad, dt_bias_pad)


def _delta_apply(u, w, qe, qk_rows, kdt, v_place, state, decay_last, w_norm):
    n = range(len(state))
    sb = [s.astype(BF16) for s in state]
    ws = [_bdot(w[i], sb[i]) for i in n]
    qs = [_bdot(qe[i], sb[i]) for i in n]
    v_pad = [v_place(u[i] - ws[i]) for i in n]
    o = [qs[i] + _bdot(qk_rows[i], v_pad[i]) for i in n]
    new_state = [state[i] * decay_last[i] + _bdot(kdt[i], v_pad[i]) for i in n]
    o = [x * lax.rsqrt(jnp.mean(x * x, axis=-1, keepdims=True) + NORM_EPS) * w_norm for x in o]
    return o, new_state


def _gdn_carry_kernel(u_ref, w_ref, qe_ref, kdt_ref, qk_ref, ba_ref, alog_ref, dtb_ref, wn_ref,
                      o_ref, s_out_ref, s_ref, *, chunk):
    t = pl.program_id(1)
    rows = u_ref.shape[0]
    n_chunks = rows // chunk

    @pl.when(t == 0)
    def _():
        s_ref[...] = jnp.zeros_like(s_ref)

    _, g_cum = _gate_terms(ba_ref, alog_ref, dtb_ref, chunk)
    w_norm = wn_ref[...]
    zeros = jnp.zeros((chunk, HEAD_DIM), BF16)
    heads = range(N_HEADS)
    sls = [slice(h * HEAD_DIM, (h + 1) * HEAD_DIM) for h in heads]
    state = [s_ref[h] for h in heads]
    for c in range(n_chunks):
        rs = slice(c * chunk, (c + 1) * chunk)
        last = (c + 1) * chunk - 1

        def v_place(v_new, c=c):
            return jnp.concatenate([zeros] * c + [v_new.astype(BF16)] + [zeros] * (n_chunks - 1 - c), axis=0)

        o, state = _delta_apply(
            [u_ref[rs, sl] for sl in sls], [w_ref[rs, sl] for sl in sls], [qe_ref[rs, sl] for sl in sls],
            [qk_ref[0, h * HEAD_DIM + c * chunk:h * HEAD_DIM + (c + 1) * chunk, :] for h in heads],
            [kdt_ref[0, sl, :] for sl in sls], v_place, state,
            [jnp.exp(g_cum[last:last + 1, N_HEADS + h:N_HEADS + h + 1]) for h in heads], w_norm)
        for h in heads:
            o_ref[rs, sls[h]] = o[h].astype(o_ref.dtype)
    for h in heads:
        s_ref[h] = state[h]

    @pl.when(t == pl.num_programs(1) - 1)
    def _():
        s_out_ref[0] = s_ref[...]


def gdn_carry(u, w, qe, kdt, qk, ba, a_log_pad, dt_bias_pad, w_norm, n_seq, chunk):
    T = u.shape[0]
    rows = LANES
    nblk = T // n_seq // rows
    vec = pl.BlockSpec((1, LANES), lambda b, t: (0, 0))
    tok = pl.BlockSpec((rows, WIDTH), lambda b, t: (b * nblk + t, 0))
    blk = pl.BlockSpec((1, WIDTH, rows), lambda b, t: (b * nblk + t, 0, 0))
    return pl.pallas_call(
        functools.partial(_gdn_carry_kernel, chunk=chunk),
        out_shape=[jax.ShapeDtypeStruct((T, WIDTH), BF16),
                   jax.ShapeDtypeStruct((n_seq, N_HEADS, HEAD_DIM, HEAD_DIM), F32)],
        grid=(n_seq, nblk),
        in_specs=[tok, tok, tok, blk, blk,
                  pl.BlockSpec((rows, LANES), lambda b, t: (b * nblk + t, 0)), vec, vec, vec],
        out_specs=[tok, pl.BlockSpec((1, N_HEADS, HEAD_DIM, HEAD_DIM), lambda b, t: (b, 0, 0, 0))],
        scratch_shapes=[pltpu.VMEM((N_HEADS, HEAD_DIM, HEAD_DIM), F32)],
        compiler_params=_params("parallel", "arbitrary"),
        name="gdn_carry",
    )(u, w, qe, kdt, qk, ba, a_log_pad, dt_bias_pad, w_norm.reshape(1, HEAD_DIM))


def _gdn_states_kernel(u_ref, w_ref, qe_ref, kdt_ref, qk_ref, ba_ref, s_in_ref, alog_ref, dtb_ref, wn_ref,
                       o_ref, s_out_ref, gcum_s, *, chunk, hb):
    hg = pl.program_id(1)
    rows = u_ref.shape[0]
    n_seq = rows // chunk
    _, g_cum = _gate_terms(ba_ref, alog_ref, dtb_ref, chunk)
    gcum_s[...] = g_cum
    w_norm = wn_ref[...]
    row_seq = lax.broadcasted_iota(jnp.int32, (rows, HEAD_DIM), 0) // chunk
    lane = lax.broadcasted_iota(jnp.int32, (1, LANES), 1)

    heads = range(hb)
    sls = [slice(hh * HEAD_DIM, (hh + 1) * HEAD_DIM) for hh in heads]

    def body(s, carry):
        rs = pl.ds(pl.multiple_of(s * chunk, chunk), chunk)
        g_last_row = gcum_s[pl.ds(s * chunk + (chunk - 1), 1), :]

        def v_place(v_new):
            tiled = jnp.concatenate([v_new] * n_seq, axis=0)
            return jnp.where(row_seq == s, tiled, 0.0).astype(BF16)

        o, state = _delta_apply(
            [u_ref[rs, sl] for sl in sls], [w_ref[rs, sl] for sl in sls], [qe_ref[rs, sl] for sl in sls],
            [qk_ref[0, pl.ds(pl.multiple_of(hh * HEAD_DIM + s * chunk, chunk), chunk), :] for hh in heads],
            [kdt_ref[0, sl, :] for sl in sls], v_place, [s_in_ref[s, hh] for hh in heads],
            [jnp.exp(jnp.sum(jnp.where(lane == hg * hb + hh + N_HEADS, g_last_row, 0.0), axis=-1, keepdims=True))
             for hh in heads], w_norm)
        for hh in heads:
            o_ref[rs, sls[hh]] = o[hh]
            s_out_ref[s, hh] = state[hh]
        return carry

    lax.fori_loop(0, n_seq, body, 0)


def gdn_states(u, w, qe, kdt, qk, ba, state, a_log_pad, dt_bias_pad, w_norm, chunk, hb=8):
    T = u.shape[0]
    rows = LANES
    seq_blk = rows // chunk
    n_hg = N_HEADS // hb
    wblk = hb * HEAD_DIM
    vec = pl.BlockSpec((1, LANES), lambda i, h: (0, 0))
    tok = pl.BlockSpec((rows, wblk), lambda i, h: (i, h))
    blk = pl.BlockSpec((1, wblk, rows), lambda i, h: (i, h, 0))
    st = pl.BlockSpec((seq_blk, hb, HEAD_DIM, HEAD_DIM), lambda i, h: (i, h, 0, 0))
    return pl.pallas_call(
        functools.partial(_gdn_states_kernel, chunk=chunk, hb=hb),
        out_shape=[jax.ShapeDtypeStruct((T, WIDTH), F32), jax.ShapeDtypeStruct(state.shape, F32)],
        grid=(T // rows, n_hg),
        in_specs=[tok, tok, tok, blk, blk, pl.BlockSpec((rows, LANES), lambda i, h: (i, 0)), st, vec, vec, vec],
        out_specs=[tok, st],
        scratch_shapes=[pltpu.VMEM((rows, LANES), F32)],
        compiler_params=_params("parallel", "parallel"),
        name="gdn_states",
    )(u, w, qe, kdt, qk, ba, state, a_log_pad, dt_bias_pad, w_norm.reshape(1, HEAD_DIM))


def _merge_kernel(oa_ref, za_ref, od_ref, zb_ref, ga_ref, gb_ref, wa_ref, wd_ref, o_ref):
    f32 = lambda ref: ref[...].astype(F32)
    a = (f32(oa_ref) * _silu(f32(za_ref))).astype(BF16)
    d = (f32(od_ref) * _silu(f32(zb_ref))).astype(BF16)
    pa = jnp.dot(a, wa_ref[...], preferred_element_type=F32)
    pd = jnp.dot(d, wd_ref[...], preferred_element_type=F32)
    o_ref[...] = (_sigmoid(f32(ga_ref)) * pa + _sigmoid(f32(gb_ref)) * pd).astype(o_ref.dtype)


def merge_branches(oa, za, od, zb, ga, gb, w_a, w_d, tm):
    T = oa.shape[0]
    act = pl.BlockSpec((tm, WIDTH), lambda i: (i, 0))
    wgt = pl.BlockSpec((WIDTH, D_MODEL), lambda i: (0, 0), pipeline_mode=pl.Buffered(1))
    return pl.pallas_call(
        _merge_kernel,
        out_shape=jax.ShapeDtypeStruct((T, D_MODEL), BF16),
        grid=(T // tm,),
        in_specs=[act, act, act, act, act, act, wgt, wgt],
        out_specs=pl.BlockSpec((tm, D_MODEL), lambda i: (i, 0)),
        compiler_params=_params("parallel"),
        name="merge_branches",
    )(oa, za, od, zb, ga, gb, w_a, w_d)


def _out_kernel(m_ref, x_ref, p_ref, wo_ref, wg_ref, wp_ref, nw_ref, y_ref):
    x1 = x_ref[...] + jnp.dot(m_ref[...], wo_ref[...], preferred_element_type=F32)
    ms = jnp.mean(x1 * x1, axis=-1, keepdims=True)
    hn = (x1 * lax.rsqrt(ms + NORM_EPS) * nw_ref[...]).astype(BF16)
    gate = _sigmoid(jnp.dot(hn, wg_ref[...], preferred_element_type=F32))
    emb = jnp.dot(p_ref[...].astype(BF16), wp_ref[...], preferred_element_type=F32)
    y_ref[...] = x1 + gate * emb


def output_block(merged, x, p, w_out, w_gate, w_proj, norm_w, tm=256):
    T = x.shape[0]
    const = lambda shape: pl.BlockSpec(shape, lambda i: (0, 0), pipeline_mode=pl.Buffered(1))
    return pl.pallas_call(
        _out_kernel,
        out_shape=jax.ShapeDtypeStruct((T, D_MODEL), F32),
        grid=(T // tm,),
        in_specs=[pl.BlockSpec((tm, D_MODEL), lambda i: (i, 0)),
                  pl.BlockSpec((tm, D_MODEL), lambda i: (i, 0)),
                  pl.BlockSpec((tm, PLE_DIM), lambda i: (i, 0)),
                  const((D_MODEL, D_MODEL)), const((D_MODEL, D_MODEL)), const((PLE_DIM, D_MODEL)),
                  const((1, D_MODEL))],
        out_specs=pl.BlockSpec((tm, D_MODEL), lambda i: (i, 0)),
        compiler_params=_params("parallel"),
        name="output_block",
    )(merged, x, p, w_out, w_gate, w_proj, norm_w.reshape(1, D_MODEL))


def _layer(x, p, n_seq, seq_len, pos, lw, attend, conv_halo, state):
    T = x.shape[0]
    h = rmsnorm_bf16(x, lw["norm_w"])
    proj = lambda start, n, dt=F32: matmul_cols(h, lw["w_in"], start, n, IN_PROJ_ROWS, 1024, dt)
    tail = lambda start, n, tn, dt: matmul_cols(h, lw["w_in_tail"], start, n, IN_PROJ_ROWS, tn, dt)
    qk = proj(0, 2 * WIDTH)
    va = proj(2 * WIDTH, WIDTH)
    za = proj(3 * WIDTH, WIDTH, BF16)
    conv_in = proj(4 * WIDTH, CONV_CH)
    zb = proj(4 * WIDTH + CONV_CH, WIDTH, BF16)
    ga = tail(0, D_MODEL, 1024, BF16)
    gb = tail(D_MODEL, D_MODEL, 1024, BF16)
    ba = tail(2 * D_MODEL, LANES, LANES, F32)

    cos, sin = rope_tables(pos)
    if cos.shape[0] < MOBA_BLOCK:
        reps = MOBA_BLOCK // cos.shape[0]
        cos, sin = jnp.tile(cos, (reps, 1)), jnp.tile(sin, (reps, 1))
    qa, ka, k_out, v_out, *kmean = qk_prep(qk, va, lw["q_norm_w"], lw["k_norm_w"], cos, sin,
                                           with_kmean=attend == "prompt")
    if attend == "prompt":
        oa = moba_prompt(qa, ka, va, kmean[0].reshape(n_seq, seq_len // MOBA_BLOCK, WIDTH), n_seq, seq_len)
    else:
        oa = moba_sample(qa, ka, va, lw["cache_k"], lw["cache_v"], lw["page_table"], seq_len)

    qkv, new_conv = gdn_prep(conv_in.reshape(n_seq, seq_len, CONV_CH), lw["conv_w"], conv_halo,
                             n_seq_blk=1 if conv_halo is None else min(n_seq, 128))
    qkv = qkv.reshape(T, CONV_CH)
    if state is None:
        stage1 = gdn_stage1(qkv, ba, lw["a_log"], lw["dt_bias"], GDN_CHUNK, BF16)
        od, new_state = gdn_carry(*stage1, ba, lw["a_log"], lw["dt_bias"], lw["dn_norm_w"], n_seq, GDN_CHUNK)
    else:
        stage1 = gdn_stage1(qkv, ba, lw["a_log"], lw["dt_bias"], seq_len, F32)
        od, new_state = gdn_states(*stage1, ba, state, lw["a_log"], lw["dt_bias"], lw["dn_norm_w"], seq_len)

    merged = merge_branches(oa, za, od, zb, ga, gb, lw["w_o_attn"], lw["w_o_delta"],
                            tm=512 if oa.dtype == BF16 and od.dtype == BF16 else 256)
    y = output_block(merged, x, p, lw["w_out"], lw["ple_gate"], lw["ple_proj"], lw["ple_norm_w"])
    return y, k_out, v_out, new_state, new_conv


def kernel(x_prompt, x_sample, cache_k, cache_v, state_delta, state_conv, page_table, p_prompt, p_sample,
           norm_w, w_in, q_norm_w, k_norm_w, conv_w, a_log, dt_bias, dn_norm_w, w_o_attn, w_o_delta, w_out,
           ple_proj, ple_gate, ple_norm_w):
    depth = w_in.shape[0]
    B, L, _ = x_prompt.shape
    DB, DL, _ = x_sample.shape
    yp = x_prompt.reshape(B * L, D_MODEL)
    ys = x_sample.reshape(DB * DL, D_MODEL)
    outs = [[] for _ in range(8)]
    n_pool = cache_k.shape[1]
    ba_start = 4 * WIDTH + CONV_CH + WIDTH
    for i in range(depth):
        wi = w_in[i]
        w_main = wi[:, :ba_start].astype(BF16)
        w_tail = jnp.concatenate([wi[:, ba_start + 2 * N_HEADS:], wi[:, ba_start:ba_start + 2 * N_HEADS],
                                  jnp.zeros((D_MODEL, LANES - 2 * N_HEADS), wi.dtype)], axis=1).astype(BF16)
        pad = lambda v, off: jnp.zeros((1, LANES), F32).at[0, off:off + N_HEADS].set(v.astype(F32))
        lw = dict(norm_w=norm_w[i], w_in=w_main, w_in_tail=w_tail,
                  q_norm_w=q_norm_w[i], k_norm_w=k_norm_w[i], conv_w=conv_w[i],
                  a_log=pad(a_log[i], N_HEADS), dt_bias=pad(dt_bias[i], N_HEADS), dn_norm_w=dn_norm_w[i],
                  w_o_attn=w_o_attn[i].astype(BF16), w_o_delta=w_o_delta[i].astype(BF16),
                  w_out=w_out[i].astype(BF16), ple_proj=ple_proj[i].astype(BF16),
                  ple_gate=ple_gate[i].astype(BF16), ple_norm_w=ple_norm_w[i],
                  cache_k=cache_k[i].reshape(n_pool, PAGE_ROWS, HEAD_DIM),
                  cache_v=cache_v[i].reshape(n_pool, PAGE_ROWS, HEAD_DIM), page_table=page_table)
        yp, kp, vp, sp, cp = _layer(yp, p_prompt[i].reshape(B * L, PLE_DIM), B, L, jnp.arange(L), lw,
                                    "prompt", None, None)
        halo = jnp.pad(state_conv[i], ((0, 0), (8 - (CONV_WIDTH - 1), 0), (0, 0)))
        ys, ks, vs, ss, cs = _layer(ys, p_sample[i].reshape(DB * DL, PLE_DIM), DB, DL, PAST_LEN + jnp.arange(DL), lw,
                                    "sample", halo, state_delta[i])
        for lst, val in zip(outs, (kp.reshape(B, L, N_HEADS, HEAD_DIM), vp.reshape(B, L, N_HEADS, HEAD_DIM), sp, cp,
                                   ks.reshape(DB, DL, N_HEADS, HEAD_DIM), vs.reshape(DB, DL, N_HEADS, HEAD_DIM), ss, cs)):
            lst.append(val)
    stacked = [jnp.stack(v) for v in outs]
    return (yp.reshape(B, L, D_MODEL), ys.reshape(DB, DL, D_MODEL), *stacked)
```

```python
import functools
import math

import jax
import jax.numpy as jnp
from jax import lax
from jax.experimental import pallas as pl
from jax.experimental.pallas import tpu as pltpu

F32 = jnp.float32
BF16 = jnp.bfloat16
HIGHEST = lax.Precision.HIGHEST

D_MODEL = 2048
HEAD_DIM = 128
N_HEADS = 16
WIDTH = N_HEADS * HEAD_DIM
MOBA_BLOCK = 256
MOBA_TOPK = 3
ROPE_THETA = 10000.0
CONV_CH = 3 * WIDTH
CONV_WIDTH = 4
PLE_DIM = 256
GDN_CHUNK = 64
GDN_PREP_ROWS = 256
IN_PROJ_ROWS = 2048
NORM_EPS = 1e-6
PAST_LEN = 2048
PAGE_SIZE = 128
PAGE_ROWS = PAGE_SIZE * N_HEADS
LANES = 128
VMEM_LIMIT = 56 * 1024 * 1024

NT_DIMS = (((1,), (1,)), ((), ()))
TN_DIMS = (((0,), (0,)), ((), ()))


def _params(*sem):
    return pltpu.CompilerParams(dimension_semantics=sem, vmem_limit_bytes=VMEM_LIMIT)


def _sigmoid(x):
    return 1.0 / (1.0 + jnp.exp(-x))


def _silu(x):
    return x * _sigmoid(x)


def _softplus(x):
    return jnp.maximum(x, 0.0) + jnp.log(1.0 + jnp.exp(-jnp.abs(x)))


def _rmsnorm_kernel(x_ref, w_ref, o_ref):
    x = x_ref[...]
    ms = jnp.mean(x * x, axis=-1, keepdims=True)
    o_ref[...] = (x * lax.rsqrt(ms + NORM_EPS) * w_ref[...]).astype(o_ref.dtype)


def rmsnorm_bf16(x, w, tm=512):
    T, D = x.shape
    return pl.pallas_call(
        _rmsnorm_kernel,
        out_shape=jax.ShapeDtypeStruct((T, D), BF16),
        grid=(T // tm,),
        in_specs=[pl.BlockSpec((tm, D), lambda i: (i, 0)), pl.BlockSpec((1, D), lambda i: (0, 0))],
        out_specs=pl.BlockSpec((tm, D), lambda i: (i, 0)),
        compiler_params=_params("parallel"),
        name="rmsnorm_bf16",
    )(x, w.reshape(1, D))


def _mm_kernel(h_ref, w_ref, o_ref):
    o_ref[...] = jnp.dot(h_ref[...], w_ref[...], preferred_element_type=F32).astype(o_ref.dtype)


def matmul_cols(h, w, col_start, n_cols, tm, tn, out_dtype=F32):
    T, K = h.shape
    tm = min(tm, T)
    assert col_start % tn == 0 and n_cols % tn == 0 and T % tm == 0
    off = col_start // tn
    return pl.pallas_call(
        _mm_kernel,
        out_shape=jax.ShapeDtypeStruct((T, n_cols), out_dtype),
        grid=(T // tm, n_cols // tn),
        in_specs=[pl.BlockSpec((tm, K), lambda i, j: (i, 0)),
                  pl.BlockSpec((K, tn), lambda i, j: (0, off + j))],
        out_specs=pl.BlockSpec((tm, tn), lambda i, j: (i, j)),
        compiler_params=_params("parallel", "arbitrary"),
        name="in_proj",
    )(h, w)


def _qk_prep_kernel(q_ref, k_ref, v_ref, qw_ref, kw_ref, cos_ref, sin_ref, qo_ref, ko_ref, k3_ref, v3_ref, *km_ref):
    cos = cos_ref[...]
    sin = sin_ref[...]

    def norm_rope(x, w):
        ms = jnp.mean(x * x, axis=-1, keepdims=True)
        y = x * lax.rsqrt(ms + NORM_EPS) * w
        return y * cos + pltpu.roll(y, HEAD_DIM // 2, 1) * sin

    for h in range(N_HEADS):
        sl = slice(h * HEAD_DIM, (h + 1) * HEAD_DIM)
        qo_ref[:, sl] = norm_rope(q_ref[:, sl], qw_ref[...])
        kh = norm_rope(k_ref[:, sl], kw_ref[...])
        ko_ref[:, sl] = kh
        k3_ref[:, h, :] = kh
        v3_ref[:, h, :] = v_ref[:, sl]
        if km_ref:
            km_ref[0][0, :, sl] = jnp.mean(kh, axis=0, keepdims=True)


def qk_prep(qk, v, q_w, k_w, cos, sin, with_kmean):
    T = qk.shape[0]
    tm = MOBA_BLOCK
    n_pos = cos.shape[0] // tm
    flat = pl.BlockSpec((tm, WIDTH), lambda i: (i, 0))
    per_head = pl.BlockSpec((tm, N_HEADS, HEAD_DIM), lambda i: (i, 0, 0))
    out_shape = [jax.ShapeDtypeStruct((T, WIDTH), F32), jax.ShapeDtypeStruct((T, WIDTH), F32),
                 jax.ShapeDtypeStruct((T, N_HEADS, HEAD_DIM), F32), jax.ShapeDtypeStruct((T, N_HEADS, HEAD_DIM), F32)]
    out_specs = [flat, flat, per_head, per_head]
    if with_kmean:
        out_shape.append(jax.ShapeDtypeStruct((T // tm, 1, WIDTH), F32))
        out_specs.append(pl.BlockSpec((1, 1, WIDTH), lambda i: (i, 0, 0)))
    return pl.pallas_call(
        _qk_prep_kernel,
        out_shape=out_shape,
        grid=(T // tm,),
        in_specs=[flat,
                  pl.BlockSpec((tm, WIDTH), lambda i: (i, 1)),
                  flat,
                  pl.BlockSpec((1, HEAD_DIM), lambda i: (0, 0)),
                  pl.BlockSpec((1, HEAD_DIM), lambda i: (0, 0)),
                  pl.BlockSpec((tm, HEAD_DIM), lambda i: (i % n_pos, 0)),
                  pl.BlockSpec((tm, HEAD_DIM), lambda i: (i % n_pos, 0))],
        out_specs=out_specs,
        compiler_params=_params("parallel"),
        name="qk_prep",
    )(qk, qk, v, q_w.reshape(1, HEAD_DIM), k_w.reshape(1, HEAD_DIM), cos, sin)


def rope_tables(pos):
    half = HEAD_DIM // 2
    inv_freq = jnp.power(ROPE_THETA, -jnp.arange(half, dtype=F32) / half)
    ang = pos.astype(F32)[:, None] * inv_freq[None, :]
    cos, sin = jnp.cos(ang), jnp.sin(ang)
    return jnp.concatenate([cos, cos], -1), jnp.concatenate([-sin, sin], -1)


LOG2E = 1.4426950408889634


def _topk_block_mask(gate, n_past):
    blk = lax.broadcasted_iota(jnp.int32, gate.shape, 0)
    cand = blk < n_past
    rows = []
    for n in range(n_past):
        gn = gate[n:n + 1, :]
        beats = (gate > gn) | ((gate == gn) & (blk < n))
        rank = jnp.sum(jnp.where(cand & beats, 1.0, 0.0), axis=0, keepdims=True)
        rows.append(jnp.where(rank < MOBA_TOPK, 1.0, 0.0))
    rows += [jnp.zeros_like(rows[0])] * (gate.shape[0] - n_past)
    return jnp.concatenate(rows, axis=0)


def _moba_prompt_kernel(q_ref, k_ref, v_ref, km_ref, o_ref):
    nb = k_ref.shape[0] // MOBA_BLOCK
    kb = k_ref[...].astype(BF16)
    vb = v_ref[...].astype(BF16)
    km = km_ref[0]
    rpos = lax.broadcasted_iota(jnp.int32, (MOBA_BLOCK, MOBA_BLOCK), 0)
    cpos = lax.broadcasted_iota(jnp.int32, (MOBA_BLOCK, MOBA_BLOCK), 1)
    causal = cpos <= rpos
    def scores(c):
        q = q_ref[c * MOBA_BLOCK:(c + 1) * MOBA_BLOCK, :]
        qb = (q * (HEAD_DIM ** -0.5 * LOG2E)).astype(BF16)
        s = lax.dot_general(qb, kb[:(c + 1) * MOBA_BLOCK, :], NT_DIMS, preferred_element_type=F32)
        gate = (lax.dot_general(km, q, NT_DIMS, precision=HIGHEST, preferred_element_type=F32)
                if c > MOBA_TOPK else None)
        return s, gate

    def probs(c, s, gate):
        if gate is not None:
            keep = _topk_block_mask(gate, c)
            keep = jnp.concatenate([keep, jnp.zeros((LANES - keep.shape[0], MOBA_BLOCK), F32)], axis=0).T
            parts = [jnp.where(keep[:, n:n + 1] > 0.5, s[:, n * MOBA_BLOCK:(n + 1) * MOBA_BLOCK], -jnp.inf)
                     for n in range(c)]
        else:
            parts = [s[:, n * MOBA_BLOCK:(n + 1) * MOBA_BLOCK] for n in range(c)]
        parts.append(jnp.where(causal, s[:, c * MOBA_BLOCK:], -jnp.inf))
        s = jnp.concatenate(parts, axis=1) if c else parts[0]
        p = jnp.exp2(s - jnp.max(s, axis=-1, keepdims=True))
        return p.astype(BF16), jnp.sum(p, axis=-1, keepdims=True)

    order = list(range(nb))
    groups = [[order[i], order[nb - 1 - i]] for i in range(nb // 2)] + ([[order[nb // 2]]] if nb % 2 else [])
    for group in groups:
        sg = [scores(c) for c in group]
        pg = [probs(c, *sg[i]) for i, c in enumerate(group)]
        og = [jnp.dot(pg[i][0], vb[:(c + 1) * MOBA_BLOCK, :], preferred_element_type=F32) for i, c in enumerate(group)]
        for i, c in enumerate(group):
            o_ref[c * MOBA_BLOCK:(c + 1) * MOBA_BLOCK, :] = (og[i] * (1.0 / pg[i][1])).astype(o_ref.dtype)


def moba_prompt(q, k, v, kmean, batch, seq):
    nb = seq // MOBA_BLOCK
    assert nb <= 8
    strip = pl.BlockSpec((seq, HEAD_DIM), lambda b, h: (b, h))
    return pl.pallas_call(
        _moba_prompt_kernel,
        out_shape=jax.ShapeDtypeStruct((batch * seq, WIDTH), BF16),
        grid=(batch, N_HEADS),
        in_specs=[strip, strip, strip, pl.BlockSpec((1, nb, HEAD_DIM), lambda b, h: (b, 0, h))],
        out_specs=strip,
        compiler_params=_params("parallel", "parallel"),
        name="moba_prompt",
    )(q, k, v, kmean)


def _moba_sample_kernel(pt_ref, q_ref, kn_ref, vn_ref, *refs, blocks_per_step):
    page_refs = refs[:4 * blocks_per_step]
    o_ref, g_s, m_s, l_s, o_s = refs[4 * blocks_per_step:]
    step = pl.program_id(1)
    n_q = q_ref.shape[0]
    half = N_HEADS // 2
    pairs = range(half)
    sls = [slice(h * HEAD_DIM, (h + 1) * HEAD_DIM) for h in range(N_HEADS)]
    qs = q_ref[...]
    q_pair = [jnp.concatenate([qs[:, sls[j]], qs[:, sls[j + half]]], axis=0) for j in pairs]
    qb = [(q * (HEAD_DIM ** -0.5 * LOG2E)).astype(BF16) for q in q_pair]
    first = lax.broadcasted_iota(jnp.int32, (2 * n_q, 1), 0) < n_q
    r = lax.broadcasted_iota(jnp.int32, (2 * n_q, MOBA_BLOCK), 0)
    c = lax.broadcasted_iota(jnp.int32, (2 * n_q, MOBA_BLOCK), 1)
    same_head = (r < n_q) == (c % 2 == 0)
    even_row = lax.broadcasted_iota(jnp.int32, (8, HEAD_DIM), 0) % 2 == 0

    def pair_rows(ref, j):
        return ref[0, pl.ds(j, 2 * PAGE_SIZE, stride=half), :]

    def head_sums(k):
        acc = jnp.sum(k.reshape(k.shape[0] // 8, 8, HEAD_DIM), axis=0)
        return (jnp.sum(jnp.where(even_row, acc, 0.0), axis=0, keepdims=True),
                jnp.sum(jnp.where(even_row, 0.0, acc), axis=0, keepdims=True))

    for blk in range(blocks_per_step):
        k0_ref, k1_ref, v0_ref, v1_ref = page_refs[4 * blk:4 * blk + 4]
        n = step * blocks_per_step + blk
        k0 = [pair_rows(k0_ref, j) for j in pairs]
        k1 = [pair_rows(k1_ref, j) for j in pairs]
        s0 = [jnp.where(same_head, lax.dot_general(qb[j], k0[j].astype(BF16), NT_DIMS, preferred_element_type=F32),
                        -jnp.inf) for j in pairs]
        s1 = [jnp.where(same_head, lax.dot_general(qb[j], k1[j].astype(BF16), NT_DIMS, preferred_element_type=F32),
                        -jnp.inf) for j in pairs]
        m = [jnp.max(jnp.maximum(s0[j], s1[j]), axis=-1, keepdims=True) for j in pairs]
        p0 = [jnp.exp2(s0[j] - m[j]) for j in pairs]
        p1 = [jnp.exp2(s1[j] - m[j]) for j in pairs]
        o0 = [jnp.dot(p0[j].astype(BF16), pair_rows(v0_ref, j).astype(BF16), preferred_element_type=F32) for j in pairs]
        o1 = [jnp.dot(p1[j].astype(BF16), pair_rows(v1_ref, j).astype(BF16), preferred_element_type=F32) for j in pairs]
        for j in pairs:
            sum_a, sum_b = head_sums(k0[j] + k1[j])
            kmean = jnp.where(first, sum_a, sum_b) * (1.0 / MOBA_BLOCK)
            gate = jnp.sum(q_pair[j] * kmean, axis=-1, keepdims=True)
            l = jnp.sum(p0[j] + p1[j], axis=-1, keepdims=True)
            g_s[n, j] = jnp.broadcast_to(gate, (2 * n_q, LANES))
            m_s[n, j] = jnp.broadcast_to(m[j], (2 * n_q, LANES))
            l_s[n, j] = jnp.broadcast_to(l, (2 * n_q, LANES))
            o_s[n, j] = o0[j] + o1[j]

    @pl.when(step == pl.num_programs(1) - 1)
    def _():
        nb = g_s.shape[0]
        r = lax.broadcasted_iota(jnp.int32, (2 * n_q, LANES), 0)
        c = lax.broadcasted_iota(jnp.int32, (2 * n_q, LANES), 1)
        own_ok = ((r < n_q) & (c <= r)) | ((r >= n_q) & (c >= n_q) & (c <= r))
        pad = jnp.zeros((LANES - 2 * n_q, HEAD_DIM), F32)
        kn = [jnp.concatenate([kn_ref[:, sls[j]], kn_ref[:, sls[j + half]], pad], axis=0).astype(BF16) for j in pairs]
        vn = [jnp.concatenate([vn_ref[:, sls[j]], vn_ref[:, sls[j + half]], pad], axis=0).astype(BF16) for j in pairs]
        s_own = [jnp.where(own_ok, lax.dot_general(qb[j], kn[j], NT_DIMS, preferred_element_type=F32), -jnp.inf)
                 for j in pairs]
        m_owns = [jnp.max(s, axis=-1, keepdims=True) for s in s_own]
        p_own = [jnp.exp2(s_own[j] - m_owns[j]) for j in pairs]
        o_owns = [jnp.dot(p_own[j].astype(BF16), vn[j], preferred_element_type=F32) for j in pairs]
        for j in pairs:
            m_own = m_owns[j]
            l_own = jnp.sum(p_own[j], axis=-1, keepdims=True)
            keep = []
            for a in range(nb):
                ga = g_s[a, j]
                rank = jnp.zeros((2 * n_q, LANES), F32)
                for b in range(nb):
                    if b != a:
                        gb = g_s[b, j]
                        rank = rank + jnp.where((gb >= ga) if b < a else (gb > ga), 1.0, 0.0)
                keep.append(rank < MOBA_TOPK)
            m_tot = jnp.broadcast_to(m_own, (2 * n_q, LANES))
            for a in range(nb):
                m_tot = jnp.where(keep[a], jnp.maximum(m_tot, m_s[a, j]), m_tot)
            w_own = jnp.exp2(m_own - m_tot)
            l_tot = w_own * l_own
            o_tot = w_own * o_owns[j]
            for a in range(nb):
                w = jnp.where(keep[a], jnp.exp2(m_s[a, j] - m_tot), 0.0)
                l_tot = l_tot + w * l_s[a, j]
                o_tot = o_tot + w * o_s[a, j]
            o_tot = o_tot * (1.0 / l_tot)
            o_ref[:, sls[j]] = o_tot[:n_q]
            o_ref[:, sls[j + half]] = o_tot[n_q:]


def moba_sample(q, k_new, v_new, cache_k, cache_v, page_table, n_q, blocks_per_step=4):
    n_seq, n_pages = page_table.shape
    n_past = n_pages * PAGE_SIZE // MOBA_BLOCK
    assert MOBA_BLOCK == 2 * PAGE_SIZE and n_q == 8 and n_past % blocks_per_step == 0
    seq_spec = pl.BlockSpec((n_q, WIDTH), lambda b, s, pt: (b, 0))

    def page_spec(page_in_step):
        return pl.BlockSpec((1, PAGE_ROWS, HEAD_DIM),
                            lambda b, s, pt: (pt[b * n_pages + 2 * blocks_per_step * s + page_in_step], 0, 0))

    page_specs, page_args = [], []
    for blk in range(blocks_per_step):
        page_specs += [page_spec(2 * blk), page_spec(2 * blk + 1)] * 2
        page_args += [cache_k, cache_k, cache_v, cache_v]
    grid_spec = pltpu.PrefetchScalarGridSpec(
        num_scalar_prefetch=1,
        grid=(n_seq, n_past // blocks_per_step),
        in_specs=[seq_spec, seq_spec, seq_spec] + page_specs,
        out_specs=seq_spec,
        scratch_shapes=[pltpu.VMEM((n_past, N_HEADS // 2, 2 * n_q, LANES), F32) for _ in range(4)],
    )
    return pl.pallas_call(
        functools.partial(_moba_sample_kernel, blocks_per_step=blocks_per_step),
        out_shape=jax.ShapeDtypeStruct((n_seq * n_q, WIDTH), F32),
        grid_spec=grid_spec,
        compiler_params=_params("parallel", "arbitrary"),
        name="moba_sample",
    )(page_table.reshape(-1), q, k_new, v_new, *page_args)


def _gdn_prep_kernel(x_ref, w_ref, *refs, has_halo):
    if has_halo:
        halo_ref, y_ref, c_ref = refs
    else:
        y_ref, c_ref, xp_ref = refs
    c = pl.program_id(1)
    L = x_ref.shape[1]
    w = w_ref[...]
    c_ref[...] = x_ref[:, L - (CONV_WIDTH - 1):, :]

    def conv_silu(cur, prev):
        acc = cur * w[CONV_WIDTH - 1:CONV_WIDTH, :]
        for k in range(1, CONV_WIDTH):
            acc = acc + prev(k) * w[CONV_WIDTH - 1 - k:CONV_WIDTH - k, :]
        return _silu(acc)

    def finish(y, normalise):
        if not normalise:
            return y
        scale = jnp.where(c < N_HEADS, HEAD_DIM ** -0.5, 1.0)
        return y * (lax.rsqrt(jnp.sum(y * y, axis=-1, keepdims=True) + NORM_EPS) * scale)

    def run(normalise):
        if has_halo:
            assert L == 8
            x = x_ref[...]
            halo = halo_ref[...]
            row = lax.broadcasted_iota(jnp.int32, x.shape, 1)
            y = conv_silu(x, lambda k: jnp.where(row < k, pltpu.roll(halo, k, 1), pltpu.roll(x, k, 1)))
            y_ref[...] = finish(y, normalise)
        else:
            xp_ref[0:8, :] = jnp.zeros((8, LANES), F32)
            xp_ref[8:, :] = x_ref[0]
            for r0 in range(0, L, GDN_PREP_ROWS):
                y = conv_silu(xp_ref[8 + r0:8 + r0 + GDN_PREP_ROWS, :],
                              lambda k, r0=r0: xp_ref[8 + r0 - k:8 + r0 - k + GDN_PREP_ROWS, :])
                y_ref[0, r0:r0 + GDN_PREP_ROWS, :] = finish(y, normalise)

    pl.when(c < 2 * N_HEADS)(lambda: run(True))
    pl.when(c >= 2 * N_HEADS)(lambda: run(False))


def gdn_prep(x, conv_w, halo, n_seq_blk):
    n_seq, L, _ = x.shape
    n_strips = CONV_CH // LANES
    in_specs = [pl.BlockSpec((n_seq_blk, L, LANES), lambda s, c: (s, 0, c)),
                pl.BlockSpec((CONV_WIDTH, LANES), lambda s, c: (0, c))]
    args = [x, conv_w]
    if halo is not None:
        in_specs.append(pl.BlockSpec((n_seq_blk, 8, LANES), lambda s, c: (s, 0, c)))
        args.append(halo)
    return pl.pallas_call(
        functools.partial(_gdn_prep_kernel, has_halo=halo is not None),
        out_shape=[jax.ShapeDtypeStruct((n_seq, L, CONV_CH), F32),
                   jax.ShapeDtypeStruct((n_seq, CONV_WIDTH - 1, CONV_CH), F32)],
        grid=(n_seq // n_seq_blk, n_strips),
        in_specs=in_specs,
        out_specs=[pl.BlockSpec((n_seq_blk, L, LANES), lambda s, c: (s, 0, c)),
                   pl.BlockSpec((n_seq_blk, CONV_WIDTH - 1, LANES), lambda s, c: (s, 0, c))],
        scratch_shapes=[] if halo is not None else [pltpu.VMEM((L + 8, LANES), F32)],
        compiler_params=_params("parallel", "parallel"),
        name="gdn_prep",
    )(*args)


def _chunk_cumsum(g, chunk):
    row = lax.broadcasted_iota(jnp.int32, g.shape, 0) % chunk
    shift = 1
    while shift < chunk:
        g = g + jnp.where(row >= shift, pltpu.roll(g, shift, 0), 0.0)
        shift *= 2
    return g


def _gate_terms(ba_ref, alog_ref, dtb_ref, chunk):
    ba = ba_ref[...]
    beta = _sigmoid(ba)
    g = -jnp.exp(alog_ref[...]) * _softplus(ba + dtb_ref[...])
    return beta, _chunk_cumsum(g, chunk)


def _bdot(x, y):
    return jnp.dot(x.astype(BF16), y.astype(BF16), preferred_element_type=F32)


def _gdn_stage1_kernel(q_ref, k_ref, v_ref, ba_ref, alog_ref, dtb_ref,
                       u_ref, w_ref, qe_ref, kdt_ref, qk_ref, gt_ref, *, chunk, hb):
    hg = pl.program_id(1)
    rows = q_ref.shape[0]
    beta, g_cum = _gate_terms(ba_ref, alog_ref, dtb_ref, chunk)
    gt_ref[...] = g_cum.T
    ii = lax.broadcasted_iota(jnp.int32, (rows, rows), 0)
    jj = lax.broadcasted_iota(jnp.int32, (rows, rows), 1)
    same = (ii // chunk) == (jj // chunk)
    incl = same & (ii >= jj)
    strict = same & (ii > jj)
    is_last = jj == (ii // chunk) * chunk + (chunk - 1)
    lane = lax.broadcasted_iota(jnp.int32, (rows, LANES), 1)
    heads = range(hb)
    sls = [slice(hh * HEAD_DIM, (hh + 1) * HEAD_DIM) for hh in heads]
    b_col, g_col, gl_col, decay = [], [], [], []
    for hh in heads:
        head = hg * hb + hh
        b_col.append(jnp.sum(jnp.where(lane == head, beta, 0.0), axis=-1, keepdims=True))
        g_col.append(jnp.sum(jnp.where(lane == head + N_HEADS, g_cum, 0.0), axis=-1, keepdims=True))
        g_row = gt_ref[pl.ds(head + N_HEADS, 1), :]
        gl_col.append(jnp.sum(jnp.where(is_last, g_row, 0.0), axis=-1, keepdims=True))
        decay.append(jnp.where(incl, jnp.exp(jnp.where(incl, g_col[hh] - g_row, 0.0)), 0.0))
    kb = [k_ref[:, sl].astype(BF16) for sl in sls]
    kk = [lax.dot_general(kb[hh], kb[hh], NT_DIMS, preferred_element_type=F32) for hh in heads]
    power = [jnp.where(strict, b_col[hh] * kk[hh] * decay[hh], 0.0) for hh in heads]
    n = [-a for a in power]
    span = 2
    while span < chunk:
        power = [_bdot(p, p) for p in power]
        n = [n[hh] + power[hh] + _bdot(n[hh], power[hh]) for hh in heads]
        span *= 2
    eg = [jnp.exp(g) for g in g_col]
    rv = [v_ref[:, sls[hh]] * b_col[hh] for hh in heads]
    rk = [k_ref[:, sls[hh]] * (b_col[hh] * eg[hh]) for hh in heads]
    nv = [_bdot(n[hh], rv[hh]) for hh in heads]
    nk = [_bdot(n[hh], rk[hh]) for hh in heads]
    qk = [lax.dot_general(q_ref[:, sls[hh]].astype(BF16), kb[hh], NT_DIMS, preferred_element_type=F32) for hh in heads]
    for hh in heads:
        sl = sls[hh]
        u_ref[:, sl] = rv[hh] + nv[hh]
        w_ref[:, sl] = (rk[hh] + nk[hh]).astype(w_ref.dtype)
        qe_ref[:, sl] = (q_ref[:, sl] * eg[hh]).astype(qe_ref.dtype)
        qk_ref[0, sl, :] = (qk[hh] * decay[hh]).astype(qk_ref.dtype)
        kdt_ref[0, sl, :] = (k_ref[:, sl] * jnp.exp(gl_col[hh] - g_col[hh])).T.astype(kdt_ref.dtype)


def gdn_stage1(qkv, ba, a_log_pad, dt_bias_pad, chunk, mid_dtype, hb=8):
    T = qkv.shape[0]
    rows = LANES
    n_hg = N_HEADS // hb
    wblk = hb * HEAD_DIM
    vec = pl.BlockSpec((1, LANES), lambda i, h: (0, 0))
    tok = pl.BlockSpec((rows, wblk), lambda i, h: (i, h))
    blk = pl.BlockSpec((1, wblk, rows), lambda i, h: (i, h, 0))
    return pl.pallas_call(
        functools.partial(_gdn_stage1_kernel, chunk=chunk, hb=hb),
        out_shape=[jax.ShapeDtypeStruct((T, WIDTH), F32),
                   jax.ShapeDtypeStruct((T, WIDTH), mid_dtype),
                   jax.ShapeDtypeStruct((T, WIDTH), mid_dtype),
                   jax.ShapeDtypeStruct((T // rows, WIDTH, rows), mid_dtype),
                   jax.ShapeDtypeStruct((T // rows, WIDTH, rows), mid_dtype)],
        grid=(T // rows, n_hg),
        in_specs=[tok,
                  pl.BlockSpec((rows, wblk), lambda i, h: (i, n_hg + h)),
                  pl.BlockSpec((rows, wblk), lambda i, h: (i, 2 * n_hg + h)),
                  pl.BlockSpec((rows, LANES), lambda i, h: (i, 0)),
                  vec, vec],
        out_specs=[tok, tok, tok, blk, blk],
        scratch_shapes=[pltpu.VMEM((rows, rows), F32)],
        compiler_params=_params("parallel", "parallel"),
        name="gdn_stage1",
    )(qkv, qkv, qkv, ba, a_log_pad, dt_bias_pad)


def _delta_apply(u, w, qe, qk_rows, kdt, v_place, state, decay_last, w_norm):
    n = range(len(state))
    sb = [s.astype(BF16) for s in state]
    ws = [_bdot(w[i], sb[i]) for i in n]
    qs = [_bdot(qe[i], sb[i]) for i in n]
    v_pad = [v_place(u[i] - ws[i]) for i in n]
    o = [qs[i] + _bdot(qk_rows[i], v_pad[i]) for i in n]
    new_state = [state[i] * decay_last[i] + _bdot(kdt[i], v_pad[i]) for i in n]
    o = [x * lax.rsqrt(jnp.mean(x * x, axis=-1, keepdims=True) + NORM_EPS) * w_norm for x in o]
    return o, new_state


def _gdn_carry_kernel(u_ref, w_ref, qe_ref, kdt_ref, qk_ref, ba_ref, alog_ref, dtb_ref, wn_ref,
                      o_ref, s_out_ref, s_ref, *, chunk):
    t = pl.program_id(1)
    rows = u_ref.shape[0]
    n_chunks = rows // chunk

    @pl.when(t == 0)
    def _():
        s_ref[...] = jnp.zeros_like(s_ref)

    _, g_cum = _gate_terms(ba_ref, alog_ref, dtb_ref, chunk)
    w_norm = wn_ref[...]
    zeros = jnp.zeros((chunk, HEAD_DIM), BF16)
    heads = range(N_HEADS)
    sls = [slice(h * HEAD_DIM, (h + 1) * HEAD_DIM) for h in heads]
    state = [s_ref[h] for h in heads]
    for c in range(n_chunks):
        rs = slice(c * chunk, (c + 1) * chunk)
        last = (c + 1) * chunk - 1

        def v_place(v_new, c=c):
            return jnp.concatenate([zeros] * c + [v_new.astype(BF16)] + [zeros] * (n_chunks - 1 - c), axis=0)

        o, state = _delta_apply(
            [u_ref[rs, sl] for sl in sls], [w_ref[rs, sl] for sl in sls], [qe_ref[rs, sl] for sl in sls],
            [qk_ref[0, h * HEAD_DIM + c * chunk:h * HEAD_DIM + (c + 1) * chunk, :] for h in heads],
            [kdt_ref[0, sl, :] for sl in sls], v_place, state,
            [jnp.exp(g_cum[last:last + 1, N_HEADS + h:N_HEADS + h + 1]) for h in heads], w_norm)
        for h in heads:
            o_ref[rs, sls[h]] = o[h].astype(o_ref.dtype)
    for h in heads:
        s_ref[h] = state[h]

    @pl.when(t == pl.num_programs(1) - 1)
    def _():
        s_out_ref[0] = s_ref[...]


def gdn_carry(u, w, qe, kdt, qk, ba, a_log_pad, dt_bias_pad, w_norm, n_seq, chunk):
    T = u.shape[0]
    rows = LANES
    nblk = T // n_seq // rows
    vec = pl.BlockSpec((1, LANES), lambda b, t: (0, 0))
    tok = pl.BlockSpec((rows, WIDTH), lambda b, t: (b * nblk + t, 0))
    blk = pl.BlockSpec((1, WIDTH, rows), lambda b, t: (b * nblk + t, 0, 0))
    return pl.pallas_call(
        functools.partial(_gdn_carry_kernel, chunk=chunk),
        out_shape=[jax.ShapeDtypeStruct((T, WIDTH), BF16),
                   jax.ShapeDtypeStruct((n_seq, N_HEADS, HEAD_DIM, HEAD_DIM), F32)],
        grid=(n_seq, nblk),
        in_specs=[tok, tok, tok, blk, blk,
                  pl.BlockSpec((rows, LANES), lambda b, t: (b * nblk + t, 0)), vec, vec, vec],
        out_specs=[tok, pl.BlockSpec((1, N_HEADS, HEAD_DIM, HEAD_DIM), lambda b, t: (b, 0, 0, 0))],
        scratch_shapes=[pltpu.VMEM((N_HEADS, HEAD_DIM, HEAD_DIM), F32)],
        compiler_params=_params("parallel", "arbitrary"),
        name="gdn_carry",
    )(u, w, qe, kdt, qk, ba, a_log_pad, dt_bias_pad, w_norm.reshape(1, HEAD_DIM))


def _gdn_states_kernel(u_ref, w_ref, qe_ref, kdt_ref, qk_ref, ba_ref, s_in_ref, alog_ref, dtb_ref, wn_ref,
                       o_ref, s_out_ref, gcum_s, *, chunk, hb):
    hg = pl.program_id(1)
    rows = u_ref.shape[0]
    n_seq = rows // chunk
    _, g_cum = _gate_terms(ba_ref, alog_ref, dtb_ref, chunk)
    gcum_s[...] = g_cum
    w_norm = wn_ref[...]
    row_seq = lax.broadcasted_iota(jnp.int32, (rows, HEAD_DIM), 0) // chunk
    lane = lax.broadcasted_iota(jnp.int32, (1, LANES), 1)

    heads = range(hb)
    sls = [slice(hh * HEAD_DIM, (hh + 1) * HEAD_DIM) for hh in heads]

    def body(s, carry):
        rs = pl.ds(pl.multiple_of(s * chunk, chunk), chunk)
        g_last_row = gcum_s[pl.ds(s * chunk + (chunk - 1), 1), :]

        def v_place(v_new):
            tiled = jnp.concatenate([v_new] * n_seq, axis=0)
            return jnp.where(row_seq == s, tiled, 0.0).astype(BF16)

        o, state = _delta_apply(
            [u_ref[rs, sl] for sl in sls], [w_ref[rs, sl] for sl in sls], [qe_ref[rs, sl] for sl in sls],
            [qk_ref[0, pl.ds(pl.multiple_of(hh * HEAD_DIM + s * chunk, chunk), chunk), :] for hh in heads],
            [kdt_ref[0, sl, :] for sl in sls], v_place, [s_in_ref[s, hh] for hh in heads],
            [jnp.exp(jnp.sum(jnp.where(lane == hg * hb + hh + N_HEADS, g_last_row, 0.0), axis=-1, keepdims=True))
             for hh in heads], w_norm)
        for hh in heads:
            o_ref[rs, sls[hh]] = o[hh]
            s_out_ref[s, hh] = state[hh]
        return carry

    lax.fori_loop(0, n_seq, body, 0)


def gdn_states(u, w, qe, kdt, qk, ba, state, a_log_pad, dt_bias_pad, w_norm, chunk, hb=8):
    T = u.shape[0]
    rows = LANES
    seq_blk = rows // chunk
    n_hg = N_HEADS // hb
    wblk = hb * HEAD_DIM
    vec = pl.BlockSpec((1, LANES), lambda i, h: (0, 0))
    tok = pl.BlockSpec((rows, wblk), lambda i, h: (i, h))
    blk = pl.BlockSpec((1, wblk, rows), lambda i, h: (i, h, 0))
    st = pl.BlockSpec((seq_blk, hb, HEAD_DIM, HEAD_DIM), lambda i, h: (i, h, 0, 0))
    return pl.pallas_call(
        functools.partial(_gdn_states_kernel, chunk=chunk, hb=hb),
        out_shape=[jax.ShapeDtypeStruct((T, WIDTH), F32), jax.ShapeDtypeStruct(state.shape, F32)],
        grid=(T // rows, n_hg),
        in_specs=[tok, tok, tok, blk, blk, pl.BlockSpec((rows, LANES), lambda i, h: (i, 0)), st, vec, vec, vec],
        out_specs=[tok, st],
        scratch_shapes=[pltpu.VMEM((rows, LANES), F32)],
        compiler_params=_params("parallel", "parallel"),
        name="gdn_states",
    )(u, w, qe, kdt, qk, ba, state, a_log_pad, dt_bias_pad, w_norm.reshape(1, HEAD_DIM))


def _merge_kernel(oa_ref, za_ref, od_ref, zb_ref, ga_ref, gb_ref, wa_ref, wd_ref, o_ref):
    f32 = lambda ref: ref[...].astype(F32)
    a = (f32(oa_ref) * _silu(f32(za_ref))).astype(BF16)
    d = (f32(od_ref) * _silu(f32(zb_ref))).astype(BF16)
    pa = jnp.dot(a, wa_ref[...], preferred_element_type=F32)
    pd = jnp.dot(d, wd_ref[...], preferred_element_type=F32)
    o_ref[...] = (_sigmoid(f32(ga_ref)) * pa + _sigmoid(f32(gb_ref)) * pd).astype(o_ref.dtype)


def merge_branches(oa, za, od, zb, ga, gb, w_a, w_d, tm):
    T = oa.shape[0]
    act = pl.BlockSpec((tm, WIDTH), lambda i: (i, 0))
    wgt = pl.BlockSpec((WIDTH, D_MODEL), lambda i: (0, 0), pipeline_mode=pl.Buffered(1))
    return pl.pallas_call(
        _merge_kernel,
        out_shape=jax.ShapeDtypeStruct((T, D_MODEL), BF16),
        grid=(T // tm,),
        in_specs=[act, act, act, act, act, act, wgt, wgt],
        out_specs=pl.BlockSpec((tm, D_MODEL), lambda i: (i, 0)),
        compiler_params=_params("parallel"),
        name="merge_branches",
    )(oa, za, od, zb, ga, gb, w_a, w_d)


def _out_kernel(m_ref, x_ref, p_ref, wo_ref, wg_ref, wp_ref, nw_ref, y_ref):
    x1 = x_ref[...] + jnp.dot(m_ref[...], wo_ref[...], preferred_element_type=F32)
    ms = jnp.mean(x1 * x1, axis=-1, keepdims=True)
    hn = (x1 * lax.rsqrt(ms + NORM_EPS) * nw_ref[...]).astype(BF16)
    gate = _sigmoid(jnp.dot(hn, wg_ref[...], preferred_element_type=F32))
    emb = jnp.dot(p_ref[...].astype(BF16), wp_ref[...], preferred_element_type=F32)
    y_ref[...] = x1 + gate * emb


def output_block(merged, x, p, w_out, w_gate, w_proj, norm_w, tm=256):
    T = x.shape[0]
    const = lambda shape: pl.BlockSpec(shape, lambda i: (0, 0), pipeline_mode=pl.Buffered(1))
    return pl.pallas_call(
        _out_kernel,
        out_shape=jax.ShapeDtypeStruct((T, D_MODEL), F32),
        grid=(T // tm,),
        in_specs=[pl.BlockSpec((tm, D_MODEL), lambda i: (i, 0)),
                  pl.BlockSpec((tm, D_MODEL), lambda i: (i, 0)),
                  pl.BlockSpec((tm, PLE_DIM), lambda i: (i, 0)),
                  const((D_MODEL, D_MODEL)), const((D_MODEL, D_MODEL)), const((PLE_DIM, D_MODEL)),
                  const((1, D_MODEL))],
        out_specs=pl.BlockSpec((tm, D_MODEL), lambda i: (i, 0)),
        compiler_params=_params("parallel"),
        name="output_block",
    )(merged, x, p, w_out, w_gate, w_proj, norm_w.reshape(1, D_MODEL))


def _layer(x, p, n_seq, seq_len, pos, lw, attend, conv_halo, state):
    T = x.shape[0]
    h = rmsnorm_bf16(x, lw["norm_w"])
    proj = lambda start, n, dt=F32: matmul_cols(h, lw["w_in"], start, n, IN_PROJ_ROWS, 1024, dt)
    tail = lambda start, n, tn, dt: matmul_cols(h, lw["w_in_tail"], start, n, IN_PROJ_ROWS, tn, dt)
    qk = proj(0, 2 * WIDTH)
    va = proj(2 * WIDTH, WIDTH)
    za = proj(3 * WIDTH, WIDTH, BF16)
    conv_in = proj(4 * WIDTH, CONV_CH)
    zb = proj(4 * WIDTH + CONV_CH, WIDTH, BF16)
    ga = tail(0, D_MODEL, 1024, BF16)
    gb = tail(D_MODEL, D_MODEL, 1024, BF16)
    ba = tail(2 * D_MODEL, LANES, LANES, F32)

    cos, sin = rope_tables(pos)
    if cos.shape[0] < MOBA_BLOCK:
        reps = MOBA_BLOCK // cos.shape[0]
        cos, sin = jnp.tile(cos, (reps, 1)), jnp.tile(sin, (reps, 1))
    qa, ka, k_out, v_out, *kmean = qk_prep(qk, va, lw["q_norm_w"], lw["k_norm_w"], cos, sin,
                                           with_kmean=attend == "prompt")
    if attend == "prompt":
        oa = moba_prompt(qa, ka, va, kmean[0].reshape(n_seq, seq_len // MOBA_BLOCK, WIDTH), n_seq, seq_len)
    else:
        oa = moba_sample(qa, ka, va, lw["cache_k"], lw["cache_v"], lw["page_table"], seq_len)

    qkv, new_conv = gdn_prep(conv_in.reshape(n_seq, seq_len, CONV_CH), lw["conv_w"], conv_halo,
                             n_seq_blk=1 if conv_halo is None else min(n_seq, 128))
    qkv = qkv.reshape(T, CONV_CH)
    if state is None:
        stage1 = gdn_stage1(qkv, ba, lw["a_log"], lw["dt_bias"], GDN_CHUNK, BF16)
        od, new_state = gdn_carry(*stage1, ba, lw["a_log"], lw["dt_bias"], lw["dn_norm_w"], n_seq, GDN_CHUNK)
    else:
        stage1 = gdn_stage1(qkv, ba, lw["a_log"], lw["dt_bias"], seq_len, F32)
        od, new_state = gdn_states(*stage1, ba, state, lw["a_log"], lw["dt_bias"], lw["dn_norm_w"], seq_len)

    merged = merge_branches(oa, za, od, zb, ga, gb, lw["w_o_attn"], lw["w_o_delta"],
                            tm=512 if oa.dtype == BF16 and od.dtype == BF16 else 256)
    y = output_block(merged, x, p, lw["w_out"], lw["ple_gate"], lw["ple_proj"], lw["ple_norm_w"])
    return y, k_out, v_out, new_state, new_conv


def kernel(x_prompt, x_sample, cache_k, cache_v, state_delta, state_conv, page_table, p_prompt, p_sample,
           norm_w, w_in, q_norm_w, k_norm_w, conv_w, a_log, dt_bias, dn_norm_w, w_o_attn, w_o_delta, w_out,
           ple_proj, ple_gate, ple_norm_w):
    depth = w_in.shape[0]
    B, L, _ = x_prompt.shape
    DB, DL, _ = x_sample.shape
    yp = x_prompt.reshape(B * L, D_MODEL)
    ys = x_sample.reshape(DB * DL, D_MODEL)
    outs = [[] for _ in range(8)]
    n_pool = cache_k.shape[1]
    ba_start = 4 * WIDTH + CONV_CH + WIDTH
    for i in range(depth):
        wi = w_in[i]
        w_main = wi[:, :ba_start].astype(BF16)
        w_tail = jnp.concatenate([wi[:, ba_start + 2 * N_HEADS:], wi[:, ba_start:ba_start + 2 * N_HEADS],
                                  jnp.zeros((D_MODEL, LANES - 2 * N_HEADS), wi.dtype)], axis=1).astype(BF16)
        pad = lambda v, off: jnp.zeros((1, LANES), F32).at[0, off:off + N_HEADS].set(v.astype(F32))
        lw = dict(norm_w=norm_w[i], w_in=w_main, w_in_tail=w_tail,
                  q_norm_w=q_norm_w[i], k_norm_w=k_norm_w[i], conv_w=conv_w[i],
                  a_log=pad(a_log[i], N_HEADS), dt_bias=pad(dt_bias[i], N_HEADS), dn_norm_w=dn_norm_w[i],
                  w_o_attn=w_o_attn[i].astype(BF16), w_o_delta=w_o_delta[i].astype(BF16),
                  w_out=w_out[i].astype(BF16), ple_proj=ple_proj[i].astype(BF16),
                  ple_gate=ple_gate[i].astype(BF16), ple_norm_w=ple_norm_w[i],
                  cache_k=cache_k[i].reshape(n_pool, PAGE_ROWS, HEAD_DIM),
                  cache_v=cache_v[i].reshape(n_pool, PAGE_ROWS, HEAD_DIM), page_table=page_table)
        yp, kp, vp, sp, cp = _layer(yp, p_prompt[i].reshape(B * L, PLE_DIM), B, L, jnp.arange(L), lw,
                                    "prompt", None, None)
        halo = jnp.pad(state_conv[i], ((0, 0), (8 - (CONV_WIDTH - 1), 0), (0, 0)))
        ys, ks, vs, ss, cs = _layer(ys, p_sample[i].reshape(DB * DL, PLE_DIM), DB, DL, PAST_LEN + jnp.arange(DL), lw,
                                    "sample", halo, state_delta[i])
        for lst, val in zip(outs, (kp.reshape(B, L, N_HEADS, HEAD_DIM), vp.reshape(B, L, N_HEADS, HEAD_DIM), sp, cp,
                                   ks.reshape(DB, DL, N_HEADS, HEAD_DIM), vs.reshape(DB, DL, N_HEADS, HEAD_DIM), ss, cs)):
            lst.append(val)
    stacked = [jnp.stack(v) for v in outs]
    return (yp.reshape(B, L, D_MODEL), ys.reshape(DB, DL, D_MODEL), *stacked)
```
